```python
import math
import jax, jax.numpy as jnp
from jax import lax
import numpy as np

D_MODEL = 1024
BATCH = 8
SEQ = 8192
DEPTH = 1

GRID_W = 64
CTX_LEN = 256
CHUNK = 64

HG_HEADS = 8
HG_DK = 128
HG_DV = 128
HG_WIDTH = HG_HEADS * HG_DK

MA_INNER = 2 * D_MODEL
MA_HEAD_DIM = 64
MA_HEADS = MA_INNER // MA_HEAD_DIM
MA_GROUPS = 8
MA_STATE = 128
MA_CONV = 5
MA_XB = MA_INNER + MA_GROUPS * MA_STATE
MA_XBC = MA_XB + MA_GROUPS * MA_STATE

PEER_HEADS = 8
PEER_NKEYS = 128
PEER_EXPERTS = PEER_NKEYS * PEER_NKEYS
PEER_TOPK = 16
PEER_DKEY = 256
PEER_BLOCK = 128

DN_ALPHA = (2.0 * DEPTH) ** 0.25
DN_BETA = (8.0 * DEPTH) ** -0.25
LN_EPS = 1e-6

STATE_SIZES = (HG_WIDTH, HG_WIDTH, HG_WIDTH, MA_XB, MA_HEADS, MA_HEADS)
READ_SIZES = (HG_WIDTH, HG_WIDTH, MA_GROUPS * MA_STATE, MA_INNER, D_MODEL, D_MODEL)
STATE_DIM = sum(STATE_SIZES)
IN_DIM = STATE_DIM + sum(READ_SIZES)

kernel_name = 'hybrid_hgrn2_ssd_peer_dit_block'


def _split(a, sizes):
    return jnp.split(a, np.cumsum(sizes)[:-1].tolist(), axis=-1)


def _flip(a):
    return jnp.flip(a, axis=1)


def layer_norm(x):
    xf = x.astype(jnp.float32)
    mu = jnp.mean(xf, -1, keepdims=True)
    var = jnp.mean(jnp.square(xf - mu), -1, keepdims=True)
    return ((xf - mu) * lax.rsqrt(var + LN_EPS)).astype(x.dtype)


def rms_norm(x, w):
    xf = x.astype(jnp.float32)
    return xf * lax.rsqrt(jnp.mean(xf * xf, -1, keepdims=True) + LN_EPS) * w.astype(jnp.float32)


def modulate(x, shift, scale):
    return layer_norm(x) * (1 + scale) + shift


def post_ln(z, g, b):
    return layer_norm(z) * g + b


def to_col_major(a, rows):
    b = a.shape[0]
    return a.reshape(b, rows, GRID_W, *a.shape[2:]).swapaxes(1, 2).reshape(a.shape)


def to_row_major(a, rows):
    b = a.shape[0]
    return a.reshape(b, GRID_W, rows, *a.shape[2:]).swapaxes(1, 2).reshape(a.shape)


def centred_dwconv(a, w, bias, rows):
    b, t, ch = a.shape
    seqs = a if rows is None else a.reshape(b * GRID_W, rows, ch)
    pad = MA_CONV // 2
    y = lax.conv_general_dilated(seqs, w.T[:, None, :].astype(a.dtype), (1,), [(pad, pad)],
                                 dimension_numbers=('NWC', 'WIO', 'NWC'), feature_group_count=ch)
    return (y + bias.astype(a.dtype)).reshape(b, t, ch)


def hgrn2_forget(f_raw, lb):
    f = lb + (1 - lb) * jax.nn.sigmoid(f_raw.astype(jnp.float32))
    return jnp.log(f), 1 - f


def hgrn2_chunk_scan(q, k, v, logf, s0):
    b, t, h, _ = q.shape
    n = t // CHUNK
    causal = jnp.tril(jnp.ones((CHUNK, CHUNK), bool))

    def chunks(a):
        return a.reshape(b, n, CHUNK, h, a.shape[-1]).transpose(1, 0, 3, 2, 4)

    def step(s, inp):
        qc, kc, vc, gc = inp
        gcum = jnp.cumsum(gc, axis=2)
        decay = jnp.exp(jnp.where(causal[:, :, None],
                                  gcum[:, :, :, None, :] - gcum[:, :, None, :, :], -jnp.inf))
        att = jnp.einsum('bhtd,bhtsd,bhsd->bhts', qc, decay, kc)
        glast = gcum[:, :, -1:, :]
        o = (jnp.einsum('bhts,bhsv->bhtv', att, vc)
             + jnp.einsum('bhtd,bhdv->bhtv', qc * jnp.exp(gcum), s))
        s = (jnp.exp(glast[:, :, 0, :, None]) * s
             + jnp.einsum('bhsd,bhsv->bhdv', kc * jnp.exp(glast - gcum), vc))
        return s, o

    s_fin, o = lax.scan(step, s0, (chunks(q), chunks(k), chunks(v), chunks(logf)))
    return o.transpose(1, 0, 3, 2, 4).reshape(b, t, h, v.shape[-1]), s_fin


def hgrn2_final_state(k, v, logf):
    gcum = jnp.cumsum(logf, axis=1)
    return jnp.einsum('bthd,bthv->bhdv', k * jnp.exp(gcum[:, -1:] - gcum), v)


def ssd_chunk_scan(xh, dt, a, bm, cm, h0):
    b, t, h, p = xh.shape
    g, nst = bm.shape[2], bm.shape[3]
    hpg = h // g
    n = t // CHUNK
    causal = jnp.tril(jnp.ones((CHUNK, CHUNK), bool))
    xdt = (xh * dt[..., None]).reshape(b, n, CHUNK, h, p).transpose(1, 0, 2, 3, 4)
    loga = (dt * a).reshape(b, n, CHUNK, h).transpose(1, 0, 3, 2)
    bch = bm.reshape(b, n, CHUNK, g, nst).transpose(1, 0, 2, 3, 4)
    cch = cm.reshape(b, n, CHUNK, g, nst).transpose(1, 0, 2, 3, 4)

    def step(hs, inp):
        xc, lc, bc, cc = inp
        lcum = jnp.cumsum(lc, axis=-1)
        lmat = jnp.exp(jnp.where(causal, lcum[..., :, None] - lcum[..., None, :], -jnp.inf))
        lmat = lmat.reshape(b, g, hpg, CHUNK, CHUNK)
        xg = xc.reshape(b, CHUNK, g, hpg, p)
        cb = jnp.einsum('btgn,bsgn->bgts', cc, bc)
        y_intra = jnp.einsum('bgts,bgjts,bsgjp->btgjp', cb, lmat, xg)
        hg = hs.reshape(b, g, hpg, p, nst)
        y_inter = jnp.einsum('btgn,bgjpn,bgjt->btgjp', cc, hg, jnp.exp(lcum).reshape(b, g, hpg, CHUNK))
        llast = lcum[..., -1]
        dec_out = jnp.exp(llast[..., None] - lcum).reshape(b, g, hpg, CHUNK)
        hs = (jnp.exp(llast)[..., None, None] * hs
              + jnp.einsum('bsgn,bgjs,bsgjp->bgjpn', bc, dec_out, xg).reshape(b, h, p, nst))
        return hs, (y_intra + y_inter).reshape(b, CHUNK, h, p)

    h_fin, y = lax.scan(step, h0, (xdt, loga, bch, cch))
    return y.transpose(1, 0, 2, 3, 4).reshape(b, t, h, p), h_fin


def ssd_final_state(xh, dt, a, bm):
    b, t, h, p = xh.shape
    g, nst = bm.shape[2], bm.shape[3]
    lcum = jnp.cumsum(dt * a, axis=1)
    wgt = dt * jnp.exp(lcum[:, -1:] - lcum)
    xg = (xh * wgt[..., None]).reshape(b, t, g, h // g, p)
    return jnp.einsum('btgn,btgjp->bgjpn', bm, xg).reshape(b, h, p, nst)


def state_side(hs, lw, lb, rows):
    b, t, _ = hs.shape
    f_f, f_b, iv, xb, dt_f, dt_b = _split(hs, STATE_SIZES)
    heads = lambda a: a.reshape(b, t, HG_HEADS, a.shape[-1] // HG_HEADS)
    logf_f, k_f = hgrn2_forget(f_f, lb[0])
    logf_b, k_b = hgrn2_forget(f_b, lb[1])
    if rows is not None:
        xb, dt_f, dt_b = to_col_major(xb, rows), to_col_major(dt_f, rows), to_col_major(dt_b, rows)
    xb = jax.nn.silu(centred_dwconv(xb, lw['ma_conv_w'][:MA_XB], lw['ma_conv_b'][:MA_XB], rows))
    xh, bm = _split(xb, (MA_INNER, MA_GROUPS * MA_STATE))
    dtb = lw['ma_dt_bias'].astype(jnp.float32)
    dt_f = jax.nn.softplus(dt_f.astype(jnp.float32) + dtb[0])
    dt_b = jax.nn.softplus(dt_b.astype(jnp.float32) + dtb[1])
    hg = (heads(k_f), heads(logf_f), heads(k_b), heads(logf_b), heads(iv))
    ma = (xh.reshape(b, t, MA_HEADS, MA_HEAD_DIM), bm.reshape(b, t, MA_GROUPS, MA_STATE), dt_f, dt_b)
    return hg, ma


def context_states(uc, lw, lb):
    hs = jnp.einsum('btd,de->bte', uc, lw['w_in'][:, :STATE_DIM])
    (k_f, lf_f, k_b, lf_b, iv), (xh, bm, dt_f, dt_b) = state_side(hs, lw, lb, None)
    a = -jnp.exp(lw['ma_a_log'].astype(jnp.float32))
    return (hgrn2_final_state(k_f, iv, lf_f),
            hgrn2_final_state(_flip(k_b), _flip(iv), _flip(lf_b)),
            ssd_final_state(xh, dt_f, a[0], bm),
            ssd_final_state(_flip(xh), _flip(dt_b), a[1], _flip(bm)))


def token_mix(u, lw, lb, s_init, rows):
    b, t, _ = u.shape
    proj = jnp.einsum('btd,de->bte', u, lw['w_in'])
    (k_f, lf_f, k_b, lf_b, iv), (xh, bm, dt_f, dt_b) = state_side(proj[..., :STATE_DIM], lw, lb, rows)
    q, og, cm, z, ga, gb = _split(proj[..., STATE_DIM:], READ_SIZES)
    q = jax.nn.silu(q).reshape(b, t, HG_HEADS, HG_DK)
    o_f, s_hf = hgrn2_chunk_scan(q, k_f, iv, lf_f, s_init[0])
    o_b, s_hb = hgrn2_chunk_scan(_flip(q), _flip(k_b), _flip(iv), _flip(lf_b), s_init[1])
    o = rms_norm(o_f + _flip(o_b), lw['hg_norm_w']).astype(u.dtype)
    o = o * jax.nn.silu(og.reshape(b, t, HG_HEADS, HG_DV))
    y_a = jnp.einsum('bte,ed->btd', o.reshape(b, t, HG_WIDTH), lw['w_branch_a'])
    if rows is not None:
        cm = to_col_major(cm, rows)
    cm = jax.nn.silu(centred_dwconv(cm, lw['ma_conv_w'][MA_XB:], lw['ma_conv_b'][MA_XB:], rows))
    cm = cm.reshape(b, t, MA_GROUPS, MA_STATE)
    a = -jnp.exp(lw['ma_a_log'].astype(jnp.float32))
    y_f, h_f = ssd_chunk_scan(xh, dt_f, a[0], bm, cm, s_init[2])
    y_b, h_b = ssd_chunk_scan(_flip(xh), _flip(dt_b), a[1], _flip(bm), _flip(cm), s_init[3])
    y = y_f + _flip(y_b) + xh * lw['ma_d'][:, None]
    y = y.reshape(b, t, MA_INNER).astype(u.dtype)
    if rows is not None:
        y = to_row_major(y, rows)
    yz = (y * jax.nn.silu(z)).reshape(b, t, MA_GROUPS, MA_INNER // MA_GROUPS)
    yz = (rms_norm(yz, jnp.ones((), jnp.float32)).reshape(b, t, MA_INNER) * lw['ma_norm_w']).astype(u.dtype)
    y_b2 = jnp.einsum('bte,ed->btd', yz, lw['w_branch_b'])
    m = jax.nn.sigmoid(ga) * y_a + jax.nn.sigmoid(gb) * y_b2
    out = jnp.einsum('btd,de->bte', m, lw['w_out'])
    return out, (s_hf, s_hb, h_f, h_b)


def peer_ffn(u, wq, subkeys, table_u, table_v):
    b, t, d = u.shape
    half = PEER_DKEY // 2

    def block(tb):
        q = jnp.einsum('pd,de->pe', tb, wq).reshape(-1, PEER_HEADS, 2, half)
        s = jnp.einsum('phik,hink->phin', q, subkeys).astype(jnp.float32)
        s1, i1 = lax.top_k(s[:, :, 0], PEER_TOPK)
        s2, i2 = lax.top_k(s[:, :, 1], PEER_TOPK)
        cand = (s1[..., :, None] + s2[..., None, :]).reshape(s1.shape[0], PEER_HEADS, -1)
        cidx = (i1[..., :, None] * PEER_NKEYS + i2[..., None, :]).reshape(s1.shape[0], PEER_HEADS, -1)
        top_s, pos = lax.top_k(cand, PEER_TOPK)
        idx = jnp.take_along_axis(cidx, pos, axis=-1)
        gate = jax.nn.softmax(top_s, axis=-1)
        ue = jnp.take(table_u, idx, axis=0)
        ve = jnp.take(table_v, idx, axis=0)
        act = jax.nn.gelu(jnp.einsum('pd,phkd->phk', tb, ue).astype(jnp.float32), approximate=False)
        return jnp.einsum('phk,phkd->pd', (gate * act).astype(tb.dtype), ve)

    out = lax.map(block, u.reshape(-1, PEER_BLOCK, d))
    return out.reshape(b, t, d)


def trunk_layer(x, xc, c, c_ctx, lw, lb, last):
    mod = jnp.einsum('bd,de->be', jax.nn.silu(c), lw['w_ada']) + lw['b_ada']
    sh1, sc1, g1, sh2, sc2, g2 = jnp.split(mod[:, None, :], 6, axis=-1)
    modc = jnp.einsum('d,de->e', jax.nn.silu(c_ctx), lw['w_ada']) + lw['b_ada']
    csh1, csc1, cg1, csh2, csc2, cg2 = jnp.split(modc, 6)
    rows = x.shape[1] // GRID_W
    uc = modulate(xc, csh1, csc1)
    if last:
        states = context_states(uc, lw, lb)
    else:
        bsz = xc.shape[0]
        zero = (jnp.zeros((bsz, HG_HEADS, HG_DK, HG_DV), jnp.float32),
                jnp.zeros((bsz, HG_HEADS, HG_DK, HG_DV), jnp.float32),
                jnp.zeros((bsz, MA_HEADS, MA_HEAD_DIM, MA_STATE), jnp.float32),
                jnp.zeros((bsz, MA_HEADS, MA_HEAD_DIM, MA_STATE), jnp.float32))
        mix_c, states = token_mix(uc, lw, lb, zero, None)
    mix, _ = token_mix(modulate(x, sh1, sc1), lw, lb, states, rows)
    x = post_ln(DN_ALPHA * x + g1 * mix, lw['ln1_g'], lw['ln1_b'])
    ffn = peer_ffn(modulate(x, sh2, sc2), lw['peer_wq'], lw['peer_subkeys'], lw['peer_u'], lw['peer_v'])
    x = post_ln(DN_ALPHA * x + g2 * ffn, lw['ln2_g'], lw['ln2_b'])
    if not last:
        xc = post_ln(DN_ALPHA * xc + cg1 * mix_c, lw['ln1_g'], lw['ln1_b'])
        ffn_c = peer_ffn(modulate(xc, csh2, csc2), lw['peer_wq'], lw['peer_subkeys'], lw['peer_u'], lw['peer_v'])
        xc = post_ln(DN_ALPHA * xc + cg2 * ffn_c, lw['ln2_g'], lw['ln2_b'])
    return x, xc


def setup_inputs(seed: int = 0) -> dict:
    key = jax.random.key(seed)
    ks = jax.random.split(key, 32)
    f32 = jnp.float32
    nrm = lambda k, shape, s: jax.random.normal(k, shape, f32) * s
    L, D = DEPTH, D_MODEL
    dt0 = jnp.exp(jax.random.uniform(ks[12], (L, 2, MA_HEADS), f32, math.log(1e-3), math.log(1e-1)))
    return {
        'x': nrm(ks[0], (BATCH, SEQ, D), 1.0),
        'c': nrm(ks[1], (BATCH, D), 1.0),
        'ctx': nrm(ks[2], (BATCH, CTX_LEN, D), 1.0),
        'c_ctx': nrm(ks[3], (D,), 1.0),
        'w_ada': nrm(ks[4], (L, D, 6 * D), 0.5 * D ** -0.5),
        'b_ada': nrm(ks[5], (L, 6 * D), 0.02),
        'w_in': nrm(ks[6], (L, D, IN_DIM), D ** -0.5),
        'hg_lb_logits': nrm(ks[7], (2, L + 1, HG_WIDTH), 0.5),
        'hg_norm_w': 1.0 + nrm(ks[8], (L, HG_DV), 0.02),
        'ma_conv_w': nrm(ks[9], (L, MA_XBC, MA_CONV), MA_CONV ** -0.5),
        'ma_conv_b': nrm(ks[10], (L, MA_XBC), 0.02),
        'ma_dt_bias': dt0 + jnp.log(-jnp.expm1(-dt0)),
        'ma_a_log': jnp.log(jax.random.uniform(ks[11], (L, 2, MA_HEADS), f32, 1.0, 16.0)),
        'ma_d': 1.0 + nrm(ks[13], (L, MA_HEADS), 0.02),
        'ma_norm_w': 1.0 + nrm(ks[14], (L, MA_INNER), 0.02),
        'w_branch_a': nrm(ks[15], (L, HG_WIDTH, D), HG_WIDTH ** -0.5),
        'w_branch_b': nrm(ks[16], (L, MA_INNER, D), MA_INNER ** -0.5),
        'w_out': nrm(ks[17], (L, D, D), DN_BETA * D ** -0.5),
        'ln1_g': 1.0 + nrm(ks[18], (L, D), 0.02),
        'ln1_b': nrm(ks[19], (L, D), 0.02),
        'peer_wq': nrm(ks[20], (L, D, PEER_HEADS * PEER_DKEY), D ** -0.5),
        'peer_subkeys': nrm(ks[21], (L, PEER_HEADS, 2, PEER_NKEYS, PEER_DKEY // 2), (PEER_DKEY // 2) ** -0.5),
        'peer_u': nrm(ks[22], (L, PEER_EXPERTS, D), D ** -0.5),
        'peer_v': nrm(ks[23], (L, PEER_EXPERTS, D), DN_BETA),
        'ln2_g': 1.0 + nrm(ks[24], (L, D), 0.02),
        'ln2_b': nrm(ks[25], (L, D), 0.02),
    }


def reference(x, c, ctx, c_ctx, w_ada, b_ada, w_in, hg_lb_logits, hg_norm_w, ma_conv_w, ma_conv_b,
              ma_dt_bias, ma_a_log, ma_d, ma_norm_w, w_branch_a, w_branch_b, w_out, ln1_g, ln1_b,
              peer_wq, peer_subkeys, peer_u, peer_v, ln2_g, ln2_b):
    lb_all = jnp.cumsum(jax.nn.softmax(hg_lb_logits.astype(jnp.float32), axis=1), axis=1)
    xc = ctx
    for l in range(DEPTH):
        lw = {'w_ada': w_ada[l], 'b_ada': b_ada[l], 'w_in': w_in[l], 'hg_norm_w': hg_norm_w[l],
              'ma_conv_w': ma_conv_w[l], 'ma_conv_b': ma_conv_b[l], 'ma_dt_bias': ma_dt_bias[l],
              'ma_a_log': ma_a_log[l], 'ma_d': ma_d[l], 'ma_norm_w': ma_norm_w[l],
              'w_branch_a': w_branch_a[l], 'w_branch_b': w_branch_b[l], 'w_out': w_out[l],
              'ln1_g': ln1_g[l], 'ln1_b': ln1_b[l], 'peer_wq': peer_wq[l], 'peer_subkeys': peer_subkeys[l],
              'peer_u': peer_u[l], 'peer_v': peer_v[l], 'ln2_g': ln2_g[l], 'ln2_b': ln2_b[l]}
        x, xc = trunk_layer(x, xc, c, c_ctx, lw, lb_all[:, l], l == DEPTH - 1)
    return x
```

```python
import functools

import numpy as np
import jax
import jax.numpy as jnp
from jax import lax
from jax.experimental import pallas as pl
from jax.experimental.pallas import tpu as pltpu

F32 = jnp.float32
BF16 = jnp.bfloat16
I32 = jnp.int32
HI = lax.Precision.HIGHEST

D_MODEL = 1024
GRID_W = 64
CHUNK = 64
SUB = 16
HG_HEADS = 8
HG_DK = 128
HG_WIDTH = HG_HEADS * HG_DK
MA_INNER = 2 * D_MODEL
MA_HEAD_DIM = 64
MA_HEADS = MA_INNER // MA_HEAD_DIM
MA_GROUPS = 8
MA_STATE = 128
MA_GW = MA_INNER // MA_GROUPS
MA_CONV = 5
PEER_HEADS = 8
PEER_NKEYS = 128
PEER_TOPK = 16
PEER_HALF = 128
DN_ALPHA = 2.0 ** 0.25
LN_EPS = 1e-6
NEG = -3.0e38
EXP_CLAMP = 60.0

PB_FF, PB_FB, PB_IV, PB_Q, PB_X0, PB_X1, PB_B, PB_C, PB_Z0, PB_Z1, PB_OG, PB_GA, PB_GB = range(13)
NP_BLOCKS = 13
NPROJ = NP_BLOCKS * 1024
DT_W = 128

PEER_P = 512
PEER_G = 4
PEER_PITCH = 72
VMEM_LIMIT = 56 * 1024 * 1024

NT_DIMS = (((1,), (1,)), ((), ()))
TN_DIMS = (((0,), (0,)), ((), ()))


def _cparams(sem):
    return pltpu.CompilerParams(dimension_semantics=sem, vmem_limit_bytes=VMEM_LIMIT)


def _sigmoid(x):
    return jax.nn.sigmoid(x)


def _silu(x):
    return x * jax.nn.sigmoid(x)


def _layer_norm(x):
    mu = jnp.mean(x, -1, keepdims=True)
    xc = x - mu
    var = jnp.mean(xc * xc, -1, keepdims=True)
    return xc * lax.rsqrt(var + LN_EPS)


def _ada_kernel(c_ref, w_ref, b_ref, o_ref):
    c = c_ref[...]
    o_ref[...] = jnp.dot(_silu(c), w_ref[...], precision=HI, preferred_element_type=F32) + b_ref[...]


def _ada(c_all, w_ada, b_ada):
    n, d = c_all.shape
    e = w_ada.shape[1]
    tn = 1024
    return pl.pallas_call(
        _ada_kernel,
        grid=(e // tn,),
        in_specs=[pl.BlockSpec((n, d), lambda j: (0, 0)),
                  pl.BlockSpec((d, tn), lambda j: (0, j)),
                  pl.BlockSpec((1, tn), lambda j: (0, j))],
        out_specs=pl.BlockSpec((n, tn), lambda j: (0, j)),
        out_shape=jax.ShapeDtypeStruct((n, e), F32),
        compiler_params=_cparams(("arbitrary",)),
        name="ada",
    )(c_all, w_ada, b_ada.reshape(1, e))


def _inproj_kernel(x_ref, sh_ref, sc_ref, w_ref, wdt_ref, o_ref, dt_ref, u_ref):
    @pl.when(pl.program_id(1) == 0)
    def _():
        u = _layer_norm(x_ref[...]) * (1.0 + sc_ref[0]) + sh_ref[0]
        ub = u.astype(BF16)
        u_ref[...] = ub
        dt_ref[...] = jnp.dot(ub, wdt_ref[...], preferred_element_type=F32)

    o_ref[...] = jnp.dot(u_ref[...], w_ref[...], preferred_element_type=F32).astype(BF16)


def _inproj(x2, shift, scale, w_main, w_dt, tokens_per_batch, tm):
    m, d = x2.shape
    tpb = tokens_per_batch // tm
    tn = 1024
    return pl.pallas_call(
        _inproj_kernel,
        grid=(m // tm, NPROJ // tn),
        in_specs=[pl.BlockSpec((tm, d), lambda i, j: (i, 0)),
                  pl.BlockSpec((1, 1, d), lambda i, j: (i // tpb, 0, 0)),
                  pl.BlockSpec((1, 1, d), lambda i, j: (i // tpb, 0, 0)),
                  pl.BlockSpec((d, tn), lambda i, j: (0, j)),
                  pl.BlockSpec((d, DT_W), lambda i, j: (0, 0))],
        out_specs=[pl.BlockSpec((tm, tn), lambda i, j: (i, j)),
                   pl.BlockSpec((tm, DT_W), lambda i, j: (i, 0))],
        out_shape=[jax.ShapeDtypeStruct((m, NPROJ), BF16),
                   jax.ShapeDtypeStruct((m, DT_W), F32)],
        scratch_shapes=[pltpu.VMEM((tm, d), BF16)],
        compiler_params=_cparams(("arbitrary", "arbitrary")),
        name="inproj",
    )(x2, shift, scale, w_main, w_dt)


def _hgrn_chunk(fraw, v, q, lb, tri, st_ref, reverse, need_o):
    c = fraw.shape[0]
    fg = lb + (1.0 - lb) * _sigmoid(fraw)
    logf = jnp.log(fg)
    k = 1.0 - fg
    cum = jnp.dot(tri, logf, precision=HI, preferred_element_type=F32)
    e = 0 if reverse else c - 1
    glast = cum[e:e + 1]
    kst = (k * jnp.exp(glast - cum)).astype(BF16)
    vb = v.astype(BF16)
    dec = jnp.exp(glast)
    if need_o:
        qin = (q * jnp.exp(cum)).astype(BF16)
        rows = lax.broadcasted_iota(I32, (c, c), 0)
        cols = lax.broadcasted_iota(I32, (c, c), 1)
        mask = (cols >= rows) if reverse else (cols <= rows)
        qhat, khat = [], []
        for i in range(c // SUB):
            ei = i * SUB + SUB - 1 if reverse else i * SUB
            ref = cum[ei:ei + 1] - logf[ei:ei + 1]
            blk = slice(i * SUB, (i + 1) * SUB)
            qhat.append((q[blk] * jnp.exp(cum[blk] - ref)).astype(BF16))
            khat.append((k * jnp.exp(jnp.minimum(ref - cum, EXP_CLAMP))).astype(BF16))
    outs = []
    for h in range(HG_HEADS):
        sl = slice(h * HG_DK, (h + 1) * HG_DK)
        st = st_ref[h]
        if need_o:
            o = lax.dot_general(qin[:, sl], st.astype(BF16), NT_DIMS, preferred_element_type=F32)
            att = jnp.concatenate(
                [lax.dot_general(qhat[i][:, sl], khat[i][:, sl], NT_DIMS, preferred_element_type=F32)
                 for i in range(c // SUB)], axis=0)
            att = jnp.where(mask, att, 0.0)
            o = o + jnp.dot(att.astype(BF16), vb[:, sl], preferred_element_type=F32)
            outs.append(o)
        upd = lax.dot_general(vb[:, sl], kst[:, sl], TN_DIMS, preferred_element_type=F32)
        st_ref[h] = dec[:, sl] * st + upd
    return jnp.concatenate(outs, axis=1) if need_o else None


def _hgrn_kernel(*refs, reverse, mode, nchunk):
    if mode == "state":
        f_ref, v_ref, s0_ref, lb_ref, tri_ref, sfin_ref, st_ref = refs
        q_ref = None
    elif mode == "fwd":
        f_ref, v_ref, q_ref, s0_ref, lb_ref, tri_ref, o_ref, st_ref = refs
    else:
        (f_ref, v_ref, q_ref, og_ref, op_ref, s0_ref, lb_ref, tri_ref, nw_ref, wa_ref,
         y_ref, st_ref, gs_ref) = refs
    j = pl.program_id(1)

    @pl.when(j == 0)
    def _():
        st_ref[...] = s0_ref[0]

    lb = lb_ref[...]
    tri = tri_ref[...]

    def body(ci, carry):
        cc = nchunk - 1 - ci if reverse else ci
        r0 = pl.multiple_of(cc * CHUNK, CHUNK)
        rs = pl.ds(r0, CHUNK)
        fraw = f_ref[0, rs, :].astype(F32)
        v = v_ref[0, rs, :].astype(F32)
        if mode == "state":
            _hgrn_chunk(fraw, v, None, lb, tri, st_ref, reverse, False)
            return carry
        q = _silu(q_ref[0, rs, :].astype(F32))
        o = _hgrn_chunk(fraw, v, q, lb, tri, st_ref, reverse, True)
        if mode == "fwd":
            o_ref[0, rs, :] = o
            return carry
        o = o + op_ref[0, rs, :]
        nw = nw_ref[...]
        parts = []
        for h in range(HG_HEADS):
            oh = o[:, h * HG_DK:(h + 1) * HG_DK]
            ms = jnp.mean(oh * oh, -1, keepdims=True)
            parts.append(oh * lax.rsqrt(ms + LN_EPS) * nw)
        on = jnp.concatenate(parts, axis=1)
        gs_ref[rs, :] = (on * _silu(og_ref[0, rs, :].astype(F32))).astype(BF16)
        return carry

    lax.fori_loop(0, nchunk, body, 0)

    if mode == "state":
        @pl.when(j == pl.num_programs(1) - 1)
        def _():
            sfin_ref[0] = st_ref[...]
    elif mode == "rev_out":
        y_ref[0] = jnp.dot(gs_ref[...], wa_ref[...], preferred_element_type=F32)


def _tri(n, reverse, block=None):
    block = n if block is None else block
    i = np.arange(n)
    same = (i[:, None] // block) == (i[None, :] // block)
    m = (i[None, :] >= i[:, None]) if reverse else (i[None, :] <= i[:, None])
    return jnp.asarray((m & same).astype(np.float32))


def _hgrn(proj, s0, lb, reverse, mode, tt, o_prev=None, norm_w=None, w_a=None):
    b, t, _ = proj.shape
    nt = t // tt
    hw = HG_WIDTH

    def tmap(col):
        if reverse:
            return lambda bi, j: (bi, nt - 1 - j, col)
        return lambda bi, j: (bi, j, col)

    pspec = lambda col: pl.BlockSpec((1, tt, hw), tmap(col))
    sspec = pl.BlockSpec((1, HG_HEADS, HG_DK, HG_DK), lambda bi, j: (bi, 0, 0, 0))
    cspec = lambda shape: pl.BlockSpec(shape, lambda bi, j: (0,) * len(shape))
    fcol = PB_FB if reverse else PB_FF
    tri = _tri(CHUNK, reverse)
    kern = functools.partial(_hgrn_kernel, reverse=reverse, mode=mode, nchunk=tt // CHUNK)
    st_scr = pltpu.VMEM((HG_HEADS, HG_DK, HG_DK), F32)
    if mode == "state":
        return pl.pallas_call(
            kern, grid=(b, nt),
            in_specs=[pspec(fcol), pspec(PB_IV), sspec, cspec((1, hw)), cspec((CHUNK, CHUNK))],
            out_specs=sspec,
            out_shape=jax.ShapeDtypeStruct(s0.shape, F32),
            scratch_shapes=[st_scr],
            compiler_params=_cparams(("arbitrary", "arbitrary")),
            name="hgrn_state",
        )(proj, proj, s0, lb, tri)
    if mode == "fwd":
        return pl.pallas_call(
            kern, grid=(b, nt),
            in_specs=[pspec(fcol), pspec(PB_IV), pspec(PB_Q), sspec, cspec((1, hw)), cspec((CHUNK, CHUNK))],
            out_specs=pl.BlockSpec((1, tt, hw), tmap(0)),
            out_shape=jax.ShapeDtypeStruct((b, t, hw), F32),
            scratch_shapes=[st_scr],
            compiler_params=_cparams(("arbitrary", "arbitrary")),
            name="hgrn_fwd",
        )(proj, proj, proj, s0, lb, tri)
    return pl.pallas_call(
        kern, grid=(b, nt),
        in_specs=[pspec(fcol), pspec(PB_IV), pspec(PB_Q), pspec(PB_OG),
                  pl.BlockSpec((1, tt, hw), tmap(0)), sspec, cspec((1, hw)), cspec((CHUNK, CHUNK)),
                  cspec((1, HG_DK)), cspec((hw, D_MODEL))],
        out_specs=pl.BlockSpec((1, tt, D_MODEL), tmap(0)),
        out_shape=jax.ShapeDtypeStruct((b, t, D_MODEL), F32),
        scratch_shapes=[st_scr, pltpu.VMEM((tt, hw), BF16)],
        compiler_params=_cparams(("arbitrary", "arbitrary")),
        name="hgrn_rev",
    )(proj, proj, proj, proj, o_prev, s0, lb, tri, norm_w, w_a)


def _conv_silu(xp, w, b):
    r = xp.shape[0]
    rio = lax.broadcasted_iota(I32, (r, 1), 0)
    acc = xp * w[MA_CONV // 2:MA_CONV // 2 + 1] + b
    for kk in range(MA_CONV):
        d = kk - MA_CONV // 2
        if d == 0:
            continue
        sh = pltpu.roll(xp, (-d) % r, axis=0)
        valid = jnp.logical_and(rio + d >= 0, rio + d < r)
        acc = acc + jnp.where(valid, sh, 0.0) * w[kk:kk + 1]
    return _silu(acc)


def _expand(x, e):
    hi = x.astype(BF16)
    lo = (x - hi.astype(F32)).astype(BF16)
    return (jnp.dot(hi, e, preferred_element_type=F32) + jnp.dot(lo, e, preferred_element_type=F32))


def _ssd_kernel(*refs, reverse, mode, nrows):
    if mode == "state":
        (x0_ref, x1_ref, b_ref, c_ref, dt_ref, h0_ref, cw_ref, cb_ref, dtb_ref, a_ref, e_ref,
         tri_ref, hfin_ref, hst_ref) = refs
    elif mode == "fwd":
        (x0_ref, x1_ref, b_ref, c_ref, dt_ref, h0_ref, cw_ref, cb_ref, dtb_ref, a_ref, e_ref,
         tri_ref, y_ref, hst_ref) = refs
    else:
        (x0_ref, x1_ref, b_ref, c_ref, dt_ref, h0_ref, cw_ref, cb_ref, dtb_ref, a_ref, e_ref,
         tri_ref, z0_ref, z1_ref, yf_ref, dsk_ref, nw_ref, wb_ref, y_ref, hst_ref, ys_ref) = refs
    j = pl.program_id(1)

    @pl.when(j == 0)
    def _():
        hst_ref[...] = h0_ref[0]

    cw = cw_ref[...]
    cb = cb_ref[...]
    need_y = mode != "state"
    pieces = []
    for pi, pref in enumerate((x0_ref, x1_ref, b_ref, c_ref)):
        if pi == 3 and not need_y:
            pieces.append(None)
            continue
        cs = slice(pi * 1024, (pi + 1) * 1024)
        pieces.append(_conv_silu(pref[0].astype(F32), cw[:, cs], cb[:, cs]))
    xh = jnp.concatenate(pieces[:2], axis=1)
    bm = pieces[2].astype(BF16)
    cm = pieces[3].astype(BF16) if need_y else None

    x = dt_ref[0] + dtb_ref[...]
    dt = jnp.maximum(x, 0.0) + jnp.log1p(jnp.exp(-jnp.abs(x)))
    la = dt * a_ref[...]
    e_mat = e_ref[...]
    tri = tri_ref[...]
    lane0 = MA_HEADS if reverse else 0
    rows = lax.broadcasted_iota(I32, (CHUNK, CHUNK), 0)
    cols = lax.broadcasted_iota(I32, (CHUNK, CHUNK), 1)
    mask = (cols >= rows) if reverse else (cols <= rows)
    lane128 = lax.broadcasted_iota(I32, (CHUNK, 128), 1)

    npair = nrows // 128
    for pbi in range(npair):
        pb = npair - 1 - pbi if reverse else pbi
        prow = slice(pb * 128, (pb + 1) * 128)
        lcum2 = jnp.dot(tri, la[prow], precision=HI, preferred_element_type=F32)
        lcum_t = lcum2.T
        dt_t = dt[prow].T
        for ci in range(2):
            cc = 1 - ci if reverse else ci
            crow = slice(pb * 128 + cc * CHUNK, pb * 128 + (cc + 1) * CHUNK)
            ccol = slice(cc * CHUNK, (cc + 1) * CHUNK)
            lc = lcum2[ccol]
            e = 0 if reverse else CHUNK - 1
            llast = lc[e:e + 1]
            xh_c = xh[crow]
            xb_c = xh_c.astype(BF16)
            w_out = _expand(dt[crow] * jnp.exp(llast - lc), e_mat)
            xd = (xh_c * w_out).astype(BF16)
            expl = _expand(jnp.broadcast_to(jnp.exp(llast), (8, 128)), e_mat)[0:1]
            bc = bm[crow]
            if need_y:
                dec_in = _expand(jnp.exp(lc), e_mat)
                cc_ = cm[crow]
            ys = []
            for g in range(MA_GROUPS):
                gs = slice(g * MA_STATE, (g + 1) * MA_STATE)
                gw = slice(g * MA_GW, (g + 1) * MA_GW)
                hg = hst_ref[g]
                if need_y:
                    cbg = lax.dot_general(cc_[:, gs], bc[:, gs], NT_DIMS, preferred_element_type=F32)
                    pair_out = []
                    for jp in range(2):
                        ms = []
                        for jj in range(2):
                            ln = lane0 + 4 * g + 2 * jp + jj
                            diff = lc[:, ln:ln + 1] - lcum_t[ln:ln + 1, ccol]
                            lmat = jnp.exp(jnp.where(mask, diff, NEG))
                            ms.append(cbg * lmat * dt_t[ln:ln + 1, ccol])
                        lhs = jnp.concatenate(ms, axis=0).astype(BF16)
                        xs = xb_c[:, g * MA_GW + jp * 128:g * MA_GW + (jp + 1) * 128]
                        res = jnp.dot(lhs, xs, preferred_element_type=F32)
                        pair_out.append(jnp.where(lane128 < MA_HEAD_DIM, res[:CHUNK], res[CHUNK:]))
                    y_intra = jnp.concatenate(pair_out, axis=1)
                    y_inter = jnp.dot(cc_[:, gs], hg.astype(BF16), preferred_element_type=F32)
                    ys.append(y_intra + y_inter * dec_in[:, gw])
                upd = lax.dot_general(bc[:, gs], xd[:, gw], TN_DIMS, preferred_element_type=F32)
                hst_ref[g] = expl[:, gw] * hg + upd
            if not need_y:
                continue
            y = jnp.concatenate(ys, axis=1)
            if mode == "fwd":
                y_ref[0, crow, :] = y
            else:
                ytot = yf_ref[0, crow, :] + y + xh_c * dsk_ref[...]
                zc = jnp.concatenate([z0_ref[0, crow, :], z1_ref[0, crow, :]], axis=1).astype(F32)
                yz = ytot * _silu(zc)
                nw = nw_ref[...]
                parts = []
                for g in range(MA_GROUPS):
                    gw = slice(g * MA_GW, (g + 1) * MA_GW)
                    yg = yz[:, gw]
                    ms_ = jnp.mean(yg * yg, -1, keepdims=True)
                    parts.append(yg * lax.rsqrt(ms_ + LN_EPS) * nw[:, gw])
                ys_ref[crow, :] = jnp.concatenate(parts, axis=1).astype(BF16)

    if mode == "state":
        @pl.when(j == pl.num_programs(1) - 1)
        def _():
            hfin_ref[0] = hst_ref[...]
    elif mode == "rev_out":
        y_ref[0] = jnp.dot(ys_ref[...], wb_ref[...], preferred_element_type=F32)


def _ssd(proj, dtraw, h0, consts, reverse, mode, nrows, ncols, y_f=None, extra=None):
    b, t, _ = proj.shape
    pv = proj.reshape(b, nrows, ncols * NPROJ)
    dv = dtraw.reshape(b, nrows, ncols * DT_W)
    conv_w, conv_b, dt_bias, a128, e_mat = consts

    def wmap(nblk, col):
        if reverse:
            return lambda bi, j: (bi, 0, (ncols - 1 - j) * nblk + col)
        return lambda bi, j: (bi, 0, j * nblk + col)

    pspec = lambda col: pl.BlockSpec((1, nrows, 1024), wmap(NP_BLOCKS, col))
    dspec = pl.BlockSpec((1, nrows, DT_W), wmap(1, 0))
    hspec = pl.BlockSpec((1, MA_GROUPS, MA_STATE, MA_GW), lambda bi, j: (bi, 0, 0, 0))
    cspec = lambda shape: pl.BlockSpec(shape, lambda bi, j: (0,) * len(shape))
    tri = _tri(128, reverse, block=CHUNK)
    kern = functools.partial(_ssd_kernel, reverse=reverse, mode=mode, nrows=nrows)
    base_specs = [pspec(PB_X0), pspec(PB_X1), pspec(PB_B), pspec(PB_C), dspec, hspec,
                  cspec((MA_CONV, 4096)), cspec((1, 4096)), cspec((1, DT_W)), cspec((1, DT_W)),
                  cspec((DT_W, MA_INNER)), cspec((128, 128))]
    base_args = [pv, pv, pv, pv, dv, h0, conv_w, conv_b, dt_bias, a128, e_mat, tri]
    h_scr = pltpu.VMEM((MA_GROUPS, MA_STATE, MA_GW), F32)
    sem = _cparams(("arbitrary", "arbitrary"))
    if mode == "state":
        return pl.pallas_call(
            kern, grid=(b, ncols), in_specs=base_specs, out_specs=hspec,
            out_shape=jax.ShapeDtypeStruct(h0.shape, F32), scratch_shapes=[h_scr],
            compiler_params=sem, name="ssd_state")(*base_args)
    if mode == "fwd":
        out = pl.pallas_call(
            kern, grid=(b, ncols), in_specs=base_specs,
            out_specs=pl.BlockSpec((1, nrows, MA_INNER), wmap(1, 0)),
            out_shape=jax.ShapeDtypeStruct((b, nrows, ncols * MA_INNER), F32),
            scratch_shapes=[h_scr], compiler_params=sem, name="ssd_fwd")(*base_args)
        return out
    d_skip, norm_w, w_b = extra
    out = pl.pallas_call(
        kern, grid=(b, ncols),
        in_specs=base_specs + [pspec(PB_Z0), pspec(PB_Z1),
                               pl.BlockSpec((1, nrows, MA_INNER), wmap(1, 0)),
                               cspec((1, MA_INNER)), cspec((1, MA_INNER)), cspec((MA_INNER, D_MODEL))],
        out_specs=pl.BlockSpec((1, nrows, D_MODEL), wmap(1, 0)),
        out_shape=jax.ShapeDtypeStruct((b, nrows, ncols * D_MODEL), F32),
        scratch_shapes=[h_scr, pltpu.VMEM((nrows, MA_INNER), BF16)],
        compiler_params=sem, name="ssd_rev")(*base_args, pv, pv, y_f, d_skip, norm_w, w_b)
    return out.reshape(b, t, D_MODEL)


def _merge_kernel(x_ref, ya_ref, yb_ref, ga_ref, gb_ref, g1_ref, wo_ref, lg_ref, lb_ref, o_ref):
    m = (_sigmoid(ga_ref[...].astype(F32)) * ya_ref[...]
         + _sigmoid(gb_ref[...].astype(F32)) * yb_ref[...])
    out = jnp.dot(m.astype(BF16), wo_ref[...], preferred_element_type=F32)
    z = DN_ALPHA * x_ref[...] + g1_ref[0] * out
    o_ref[...] = _layer_norm(z) * lg_ref[...] + lb_ref[...]


def _merge(x2, y_a, y_b, proj2, g1, w_out, ln_g, ln_b, tokens_per_batch, tm):
    m, d = x2.shape
    tpb = tokens_per_batch // tm
    tspec = pl.BlockSpec((tm, d), lambda i: (i, 0))
    cspec = lambda shape: pl.BlockSpec(shape, lambda i: (0,) * len(shape))
    return pl.pallas_call(
        _merge_kernel, grid=(m // tm,),
        in_specs=[tspec, tspec, tspec,
                  pl.BlockSpec((tm, d), lambda i: (i, PB_GA)),
                  pl.BlockSpec((tm, d), lambda i: (i, PB_GB)),
                  pl.BlockSpec((1, 1, d), lambda i: (i // tpb, 0, 0)),
                  cspec((d, d)), cspec((1, d)), cspec((1, d))],
        out_specs=tspec,
        out_shape=jax.ShapeDtypeStruct((m, d), F32),
        compiler_params=_cparams(("arbitrary",)),
        name="merge",
    )(x2, y_a, y_b, proj2, proj2, g1, w_out, ln_g, ln_b)


def _cand_tables():
    pairs = [(j, l) for j in range(PEER_TOPK) for l in range(PEER_TOPK) if (j + 1) * (l + 1) <= PEER_TOPK]
    n = 64
    sel_j = np.zeros((n, PEER_TOPK), np.float32)
    sel_l = np.zeros((n, PEER_TOPK), np.float32)
    pad = np.zeros((n, 1), np.float32)
    for r, (j, l) in enumerate(pairs):
        sel_j[r, j] = 1.0
        sel_l[r, l] = 1.0
    pad[len(pairs):] = NEG
    return jnp.asarray(sel_j), jnp.asarray(sel_l), jnp.asarray(pad)


def _extract_topk(s, k, val_ref, idx_ref, row0, payload=None):
    n = s.shape[0]
    rio = lax.broadcasted_iota(I32, s.shape, 0)
    for i in range(k):
        m = jnp.max(s, axis=0, keepdims=True)
        pos = jnp.min(jnp.where(s == m, rio, n), axis=0, keepdims=True)
        hit = rio == pos
        val_ref[row0 + i:row0 + i + 1, :] = m
        if payload is None:
            idx_ref[row0 + i:row0 + i + 1, :] = pos.astype(F32)
        else:
            idx_ref[row0 + i:row0 + i + 1, :] = jnp.max(jnp.where(hit, payload, -1.0), axis=0, keepdims=True)
        s = jnp.where(hit, NEG, s)


def _peer_kernel(x_ref, sh_ref, sc_ref, g2_ref, wq_ref, sk_ref, selj_ref, sell_ref, pad_ref,
                 ulo_ref, uhi_ref, vlo_ref, vhi_ref, lg_ref, lb_ref, o_ref,
                 xm_ref, acc_ref, t_ref, s1_ref, i1_ref, s2_ref, i2_ref, tv_ref, tc_ref,
                 ai_ref, bi_ref, gt_ref):
    s = pl.program_id(1)
    p = x_ref.shape[0]
    lt = 256

    @pl.when(s == 0)
    def _():
        xm = (_layer_norm(x_ref[...]) * (1.0 + sc_ref[0]) + sh_ref[0]).astype(BF16)
        xm_ref[...] = xm
        acc_ref[...] = jnp.zeros_like(acc_ref)
        q = jnp.dot(xm, wq_ref[...], preferred_element_type=F32).astype(BF16)
        selj = selj_ref[...]
        sell = sell_ref[...]
        pad = pad_ref[...]
        for ti in range(p // lt):
            ts = slice(ti * lt, (ti + 1) * lt)
            for h in range(PEER_HEADS):
                for i, (v_ref, i_ref) in enumerate(((s1_ref, i1_ref), (s2_ref, i2_ref))):
                    cidx = 2 * h + i
                    qc = q[ts, cidx * PEER_HALF:(cidx + 1) * PEER_HALF]
                    st = lax.dot_general(sk_ref[cidx], qc, NT_DIMS, preferred_element_type=F32)
                    _extract_topk(st, PEER_TOPK, v_ref, i_ref, 0)
                s1 = s1_ref[...]
                s2 = s2_ref[...]
                cand = (jnp.dot(selj, s1, precision=HI, preferred_element_type=F32)
                        + jnp.dot(sell, s2, precision=HI, preferred_element_type=F32) + pad)
                code = (jnp.dot(selj, i1_ref[...] * float(PEER_NKEYS), precision=HI, preferred_element_type=F32)
                        + jnp.dot(sell, i2_ref[...], precision=HI, preferred_element_type=F32))
                _extract_topk(cand, PEER_TOPK, tv_ref, tc_ref, h * PEER_TOPK, payload=code)
            tv = tv_ref[...]
            gates = []
            for h in range(PEER_HEADS):
                hv = tv[h * PEER_TOPK:(h + 1) * PEER_TOPK]
                ex = jnp.exp(hv - hv[0:1])
                gates.append(ex / jnp.sum(ex, axis=0, keepdims=True))
            gate = jnp.concatenate(gates, axis=0)
            code = tc_ref[...].astype(I32)
            for sub in range(lt // 128):
                ls = slice(sub * 128, (sub + 1) * 128)
                rs = slice(ti * lt + sub * 128, ti * lt + (sub + 1) * 128)
                ct = code[:, ls].astype(F32).T.astype(I32)
                ai_ref[rs, :] = jnp.right_shift(ct, 7)
                bi_ref[rs, :] = jnp.bitwise_and(ct, PEER_NKEYS - 1)
                gt_ref[rs, :] = gate[:, ls].T

        sub_iota = lax.broadcasted_iota(I32, (PEER_NKEYS, 128), 0)

        def tok(t, carry):
            arow = ai_ref[pl.ds(t, 1), :]
            brow = bi_ref[pl.ds(t, 1), :]
            grow = gt_ref[pl.ds(t, 1), :]
            at = jnp.where(arow == sub_iota, 1.0, 0.0).astype(BF16)
            rt = jnp.where(brow == sub_iota, grow, 0.0).astype(BF16)
            w = lax.dot_general(at, rt, NT_DIMS, preferred_element_type=F32)
            lo = pltpu.bitcast(w[:64], I32)
            hi = pltpu.bitcast(w[64:], I32)
            lo = jnp.right_shift(lo + 0x8000, 16)
            hi = jnp.bitwise_and(hi + 0x8000, -65536)
            r0 = pl.multiple_of(t * PEER_PITCH, 8)
            t_ref[pl.ds(r0, 64), :] = jnp.bitwise_or(lo, hi)
            return carry

        lax.fori_loop(0, p, tok, 0)

    xm = xm_ref[...]
    tot = None
    for g in range(PEER_G):
        a = s * PEER_G + g
        wp = t_ref[pl.ds(a, p, stride=PEER_PITCH), :]
        wlo = pltpu.bitcast(jnp.left_shift(wp, 16), F32)
        whi = pltpu.bitcast(jnp.bitwise_and(wp, -65536), F32)
        for wgt, u_ref, v_ref in ((wlo, ulo_ref, vlo_ref), (whi, uhi_ref, vhi_ref)):
            hval = jnp.dot(xm, u_ref[g], preferred_element_type=F32)
            act = 0.5 * hval * (1.0 + lax.erf(hval * 0.7071067811865476))
            z = (wgt * act).astype(BF16)
            d = jnp.dot(z, v_ref[g], preferred_element_type=F32)
            tot = d if tot is None else tot + d
    acc_ref[...] += tot

    @pl.when(s == pl.num_programs(1) - 1)
    def _():
        z = DN_ALPHA * x_ref[...] + g2_ref[0] * acc_ref[...]
        o_ref[...] = _layer_norm(z) * lg_ref[...] + lb_ref[...]


def _peer(x1, shift, scale, g2, wq, sk, u_t, v_c, ln_g, ln_b, tokens_per_batch):
    m, d = x1.shape
    p = PEER_P
    tpb = tokens_per_batch // p
    nsteps = 64 // PEER_G
    selj, sell, pad = _cand_tables()
    cspec = lambda shape: pl.BlockSpec(shape, lambda i, s: (0,) * len(shape))
    bspec = pl.BlockSpec((1, 1, d), lambda i, s: (i // tpb, 0, 0))
    lt = 256
    return pl.pallas_call(
        _peer_kernel, grid=(m // p, nsteps),
        in_specs=[pl.BlockSpec((p, d), lambda i, s: (i, 0)), bspec, bspec, bspec,
                  cspec(wq.shape), cspec(sk.shape), cspec(selj.shape), cspec(sell.shape), cspec(pad.shape),
                  pl.BlockSpec((PEER_G, d, 128), lambda i, s: (s, 0, 0)),
                  pl.BlockSpec((PEER_G, d, 128), lambda i, s: (s + nsteps, 0, 0)),
                  pl.BlockSpec((PEER_G, 128, d), lambda i, s: (s, 0, 0)),
                  pl.BlockSpec((PEER_G, 128, d), lambda i, s: (s + nsteps, 0, 0)),
                  cspec((1, d)), cspec((1, d))],
        out_specs=pl.BlockSpec((p, d), lambda i, s: (i, 0)),
        out_shape=jax.ShapeDtypeStruct((m, d), F32),
        scratch_shapes=[pltpu.VMEM((p, d), BF16), pltpu.VMEM((p, d), F32),
                        pltpu.VMEM((p * PEER_PITCH, 128), I32),
                        pltpu.VMEM((PEER_TOPK, lt), F32), pltpu.VMEM((PEER_TOPK, lt), F32),
                        pltpu.VMEM((PEER_TOPK, lt), F32), pltpu.VMEM((PEER_TOPK, lt), F32),
                        pltpu.VMEM((PEER_HEADS * PEER_TOPK, lt), F32),
                        pltpu.VMEM((PEER_HEADS * PEER_TOPK, lt), F32),
                        pltpu.VMEM((p, 128), I32), pltpu.VMEM((p, 128), I32), pltpu.VMEM((p, 128), F32)],
        compiler_params=_cparams(("arbitrary", "arbitrary")),
        name="peer",
    )(x1, shift, scale, g2, wq, sk, selj, sell, pad, u_t, u_t, v_c, v_c, ln_g, ln_b)


def _layer(x, c, ctx, c_ctx, lw, lb_l):
    b, t, d = x.shape
    tc = ctx.shape[1]
    rows = t // GRID_W

    npad = -(-(b + 1) // 8) * 8
    c_all = jnp.zeros((npad, d), F32).at[:b].set(c).at[b].set(c_ctx)
    mod = _ada(c_all, lw["w_ada"], lw["b_ada"])
    sh1, sc1, g1, sh2, sc2, g2 = [mod[:b, i * d:(i + 1) * d].reshape(b, 1, d) for i in range(6)]
    csh1 = jnp.broadcast_to(mod[b, 0:d], (b, 1, d))
    csc1 = jnp.broadcast_to(mod[b, d:2 * d], (b, 1, d))

    w_in = lw["w_in"]
    off = np.cumsum([0, HG_WIDTH, HG_WIDTH, HG_WIDTH, MA_INNER + MA_GROUPS * MA_STATE, MA_HEADS, MA_HEADS,
                     HG_WIDTH, HG_WIDTH, MA_GROUPS * MA_STATE, MA_INNER, D_MODEL, D_MODEL])
    o_ff, o_fb, o_iv, o_xb, o_dtf, o_dtb, o_q, o_og, o_cm, o_z, o_ga, o_gb = off[:12]
    col = lambda o, n: w_in[:, o:o + n]
    w_main = jnp.concatenate([col(o_ff, 1024), col(o_fb, 1024), col(o_iv, 1024), col(o_q, 1024),
                              col(o_xb, 3072), col(o_cm, 1024), col(o_z, 2048), col(o_og, 1024),
                              col(o_ga, 1024), col(o_gb, 1024)], axis=1).astype(BF16)
    w_dt = jnp.concatenate([col(o_dtf, 2 * MA_HEADS), jnp.zeros((d, DT_W - 2 * MA_HEADS), F32)],
                           axis=1).astype(BF16)
    lb_f = lb_l[0].reshape(1, HG_WIDTH)
    lb_b = lb_l[1].reshape(1, HG_WIDTH)
    zpad = jnp.zeros((DT_W - 2 * MA_HEADS,), F32)
    dt_bias = jnp.concatenate([lw["ma_dt_bias"].reshape(-1), zpad]).reshape(1, DT_W)
    a128 = jnp.concatenate([-jnp.exp(lw["ma_a_log"].astype(F32)).reshape(-1), zpad]).reshape(1, DT_W)
    conv_w = lw["ma_conv_w"].T
    conv_b = lw["ma_conv_b"].reshape(1, -1)
    head_of = np.arange(MA_INNER) // MA_HEAD_DIM
    e_f = jnp.asarray((np.arange(DT_W)[:, None] == head_of[None, :]).astype(np.float32)).astype(BF16)
    e_b = jnp.asarray((np.arange(DT_W)[:, None] == (head_of + MA_HEADS)[None, :]).astype(np.float32)).astype(BF16)
    consts_f = (conv_w, conv_b, dt_bias, a128, e_f)
    consts_b = (conv_w, conv_b, dt_bias, a128, e_b)
    d_skip = jnp.repeat(lw["ma_d"], MA_HEAD_DIM).reshape(1, MA_INNER)
    ma_norm_w = lw["ma_norm_w"].reshape(1, MA_INNER)
    hg_norm_w = lw["hg_norm_w"].reshape(1, HG_DK)
    w_a = lw["w_branch_a"].astype(BF16)
    w_b = lw["w_branch_b"].astype(BF16)
    w_o = lw["w_out"].astype(BF16)

    projc, dtc = _inproj(ctx.reshape(b * tc, d), csh1, csc1, w_main, w_dt, tc, tc)
    projc = projc.reshape(b, tc, NPROJ)
    dtc = dtc.reshape(b, tc, DT_W)
    zs = jnp.zeros((b, HG_HEADS, HG_DK, HG_DK), F32)
    zh = jnp.zeros((b, MA_GROUPS, MA_STATE, MA_GW), F32)
    s_hf = _hgrn(projc, zs, lb_f, False, "state", tc)
    s_hb = _hgrn(projc, zs, lb_b, True, "state", tc)
    h_f = _ssd(projc, dtc, zh, consts_f, False, "state", tc, 1)
    h_b = _ssd(projc, dtc, zh, consts_b, True, "state", tc, 1)

    x2 = x.reshape(b * t, d)
    tm = min(1024, t)
    proj, dtr = _inproj(x2, sh1, sc1, w_main, w_dt, t, tm)
    proj3 = proj.reshape(b, t, NPROJ)
    dtr = dtr.reshape(b, t, DT_W)
    tt = min(512, t)
    o_f = _hgrn(proj3, s_hf, lb_f, False, "fwd", tt)
    y_a = _hgrn(proj3, s_hb, lb_b, True, "rev_out", tt, o_prev=o_f, norm_w=hg_norm_w, w_a=w_a)
    y_f = _ssd(proj3, dtr, h_f, consts_f, False, "fwd", rows, GRID_W)
    y_b = _ssd(proj3, dtr, h_b, consts_b, True, "rev_out", rows, GRID_W, y_f=y_f,
               extra=(d_skip, ma_norm_w, w_b))
    x1 = _merge(x2, y_a.reshape(b * t, d), y_b.reshape(b * t, d), proj, g1, w_o,
                lw["ln1_g"].reshape(1, d), lw["ln1_b"].reshape(1, d), t, min(512, t))

    wq = lw["peer_wq"].astype(BF16)
    sk = lw["peer_subkeys"].reshape(PEER_HEADS * 2, PEER_NKEYS, PEER_HALF).astype(BF16)
    u_t = lw["peer_u"].reshape(PEER_NKEYS, PEER_NKEYS, d).transpose(0, 2, 1).astype(BF16)
    v_c = lw["peer_v"].reshape(PEER_NKEYS, PEER_NKEYS, d).astype(BF16)
    out = _peer(x1, sh2, sc2, g2, wq, sk, u_t, v_c, lw["ln2_g"].reshape(1, d), lw["ln2_b"].reshape(1, d), t)
    return out.reshape(b, t, d)


def kernel(x, c, ctx, c_ctx, w_ada, b_ada, w_in, hg_lb_logits, hg_norm_w, ma_conv_w, ma_conv_b, ma_dt_bias,
           ma_a_log, ma_d, ma_norm_w, w_branch_a, w_branch_b, w_out, ln1_g, ln1_b, peer_wq, peer_subkeys,
           peer_u, peer_v, ln2_g, ln2_b):
    depth = w_in.shape[0]
    assert depth == 1, "the scan states of a single (last) layer are implemented"
    lb_all = jnp.cumsum(jax.nn.softmax(hg_lb_logits.astype(F32), axis=1), axis=1)
    lw = {"w_ada": w_ada[0], "b_ada": b_ada[0], "w_in": w_in[0], "hg_norm_w": hg_norm_w[0],
          "ma_conv_w": ma_conv_w[0], "ma_conv_b": ma_conv_b[0], "ma_dt_bias": ma_dt_bias[0],
          "ma_a_log": ma_a_log[0], "ma_d": ma_d[0], "ma_norm_w": ma_norm_w[0],
          "w_branch_a": w_branch_a[0], "w_branch_b": w_branch_b[0], "w_out": w_out[0],
          "ln1_g": ln1_g[0], "ln1_b": ln1_b[0], "peer_wq": peer_wq[0], "peer_subkeys": peer_subkeys[0],
          "peer_u": peer_u[0], "peer_v": peer_v[0], "ln2_g": ln2_g[0], "ln2_b": ln2_b[0]}
    return _layer(x, c, ctx, c_ctx, lw, lb_all[:, 0])
```

```python
import functools

import numpy as np
import jax
import jax.numpy as jnp
from jax import lax
from jax.experimental import pallas as pl
from jax.experimental.pallas import tpu as pltpu

F32 = jnp.float32
BF16 = jnp.bfloat16
I32 = jnp.int32
HI = lax.Precision.HIGHEST

D_MODEL = 1024
GRID_W = 64
CHUNK = 64
SUB = 16
HG_HEADS = 8
HG_DK = 128
HG_WIDTH = HG_HEADS * HG_DK
MA_INNER = 2 * D_MODEL
MA_HEAD_DIM = 64
MA_HEADS = MA_INNER // MA_HEAD_DIM
MA_GROUPS = 8
MA_STATE = 128
MA_GW = MA_INNER // MA_GROUPS
MA_CONV = 5
PEER_HEADS = 8
PEER_NKEYS = 128
PEER_TOPK = 16
PEER_HALF = 128
DN_ALPHA = 2.0 ** 0.25
LN_EPS = 1e-6
NEG = -3.0e38
EXP_CLAMP = 60.0

PB_FF, PB_FB, PB_IV, PB_Q, PB_X0, PB_X1, PB_B, PB_C, PB_Z0, PB_Z1, PB_OG, PB_GA, PB_GB = range(13)
NP_BLOCKS = 13
NPROJ = NP_BLOCKS * 1024
DT_W = 128

PEER_P = 512
PEER_G = 4
PEER_PITCH = 72
VMEM_LIMIT = 56 * 1024 * 1024

NT_DIMS = (((1,), (1,)), ((), ()))
TN_DIMS = (((0,), (0,)), ((), ()))


def _cparams(sem):
    return pltpu.CompilerParams(dimension_semantics=sem, vmem_limit_bytes=VMEM_LIMIT)


def _sigmoid(x):
    return jax.nn.sigmoid(x)


def _silu(x):
    return x * jax.nn.sigmoid(x)


def _layer_norm(x):
    mu = jnp.mean(x, -1, keepdims=True)
    xc = x - mu
    var = jnp.mean(xc * xc, -1, keepdims=True)
    return xc * lax.rsqrt(var + LN_EPS)


def _ada_kernel(c_ref, w_ref, b_ref, o_ref):
    c = c_ref[...]
    o_ref[...] = jnp.dot(_silu(c), w_ref[...], precision=HI, preferred_element_type=F32) + b_ref[...]


def _ada(c_all, w_ada, b_ada):
    n, d = c_all.shape
    e = w_ada.shape[1]
    tn = 1024
    return pl.pallas_call(
        _ada_kernel,
        grid=(e // tn,),
        in_specs=[pl.BlockSpec((n, d), lambda j: (0, 0)),
                  pl.BlockSpec((d, tn), lambda j: (0, j)),
                  pl.BlockSpec((1, tn), lambda j: (0, j))],
        out_specs=pl.BlockSpec((n, tn), lambda j: (0, j)),
        out_shape=jax.ShapeDtypeStruct((n, e), F32),
        compiler_params=_cparams(("arbitrary",)),
        name="ada",
    )(c_all, w_ada, b_ada.reshape(1, e))


def _inproj_kernel(x_ref, sh_ref, sc_ref, w_ref, wdt_ref, o_ref, dt_ref, u_ref):
    @pl.when(pl.program_id(1) == 0)
    def _():
        u = _layer_norm(x_ref[...]) * (1.0 + sc_ref[0]) + sh_ref[0]
        ub = u.astype(BF16)
        u_ref[...] = ub
        dt_ref[...] = jnp.dot(ub, wdt_ref[...], preferred_element_type=F32)

    o_ref[...] = jnp.dot(u_ref[...], w_ref[...], preferred_element_type=F32).astype(BF16)


def _inproj(x2, shift, scale, w_main, w_dt, tokens_per_batch, tm):
    m, d = x2.shape
    tpb = tokens_per_batch // tm
    tn = 1024
    return pl.pallas_call(
        _inproj_kernel,
        grid=(m // tm, NPROJ // tn),
        in_specs=[pl.BlockSpec((tm, d), lambda i, j: (i, 0)),
                  pl.BlockSpec((1, 1, d), lambda i, j: (i // tpb, 0, 0)),
                  pl.BlockSpec((1, 1, d), lambda i, j: (i // tpb, 0, 0)),
                  pl.BlockSpec((d, tn), lambda i, j: (0, j)),
                  pl.BlockSpec((d, DT_W), lambda i, j: (0, 0))],
        out_specs=[pl.BlockSpec((tm, tn), lambda i, j: (i, j)),
                   pl.BlockSpec((tm, DT_W), lambda i, j: (i, 0))],
        out_shape=[jax.ShapeDtypeStruct((m, NPROJ), BF16),
                   jax.ShapeDtypeStruct((m, DT_W), F32)],
        scratch_shapes=[pltpu.VMEM((tm, d), BF16)],
        compiler_params=_cparams(("arbitrary", "arbitrary")),
        name="inproj",
    )(x2, shift, scale, w_main, w_dt)


def _hgrn_chunk(fraw, v, q, lb, tri, st_ref, reverse, need_o):
    c = fraw.shape[0]
    fg = lb + (1.0 - lb) * _sigmoid(fraw)
    logf = jnp.log(fg)
    k = 1.0 - fg
    cum = jnp.dot(tri, logf, precision=HI, preferred_element_type=F32)
    e = 0 if reverse else c - 1
    glast = cum[e:e + 1]
    kst = (k * jnp.exp(glast - cum)).astype(BF16)
    vb = v.astype(BF16)
    dec = jnp.exp(glast)
    if need_o:
        qin = (q * jnp.exp(cum)).astype(BF16)
        rows = lax.broadcasted_iota(I32, (c, c), 0)
        cols = lax.broadcasted_iota(I32, (c, c), 1)
        mask = (cols >= rows) if reverse else (cols <= rows)
        qhat, khat = [], []
        for i in range(c // SUB):
            ei = i * SUB + SUB - 1 if reverse else i * SUB
            ref = cum[ei:ei + 1] - logf[ei:ei + 1]
            blk = slice(i * SUB, (i + 1) * SUB)
            qhat.append((q[blk] * jnp.exp(cum[blk] - ref)).astype(BF16))
            khat.append((k * jnp.exp(jnp.minimum(ref - cum, EXP_CLAMP))).astype(BF16))
    outs = []
    for h in range(HG_HEADS):
        sl = slice(h * HG_DK, (h + 1) * HG_DK)
        st = st_ref[h]
        if need_o:
            o = lax.dot_general(qin[:, sl], st.astype(BF16), NT_DIMS, preferred_element_type=F32)
            att = jnp.concatenate(
                [lax.dot_general(qhat[i][:, sl], khat[i][:, sl], NT_DIMS, preferred_element_type=F32)
                 for i in range(c // SUB)], axis=0)
            att = jnp.where(mask, att, 0.0)
            o = o + jnp.dot(att.astype(BF16), vb[:, sl], preferred_element_type=F32)
            outs.append(o)
        upd = lax.dot_general(vb[:, sl], kst[:, sl], TN_DIMS, preferred_element_type=F32)
        st_ref[h] = dec[:, sl] * st + upd
    return jnp.concatenate(outs, axis=1) if need_o else None


def _hgrn_kernel(*refs, reverse, mode, nchunk):
    if mode == "state":
        f_ref, v_ref, s0_ref, lb_ref, tri_ref, sfin_ref, st_ref = refs
        q_ref = None
    elif mode == "fwd":
        f_ref, v_ref, q_ref, s0_ref, lb_ref, tri_ref, o_ref, st_ref = refs
    else:
        (f_ref, v_ref, q_ref, og_ref, op_ref, s0_ref, lb_ref, tri_ref, nw_ref, wa_ref,
         y_ref, st_ref, gs_ref) = refs
    j = pl.program_id(1)

    @pl.when(j == 0)
    def _():
        st_ref[...] = s0_ref[0]

    lb = lb_ref[...]
    tri = tri_ref[...]

    def body(ci, carry):
        cc = nchunk - 1 - ci if reverse else ci
        r0 = pl.multiple_of(cc * CHUNK, CHUNK)
        rs = pl.ds(r0, CHUNK)
        fraw = f_ref[0, rs, :].astype(F32)
        v = v_ref[0, rs, :].astype(F32)
        if mode == "state":
            _hgrn_chunk(fraw, v, None, lb, tri, st_ref, reverse, False)
            return carry
        q = _silu(q_ref[0, rs, :].astype(F32))
        o = _hgrn_chunk(fraw, v, q, lb, tri, st_ref, reverse, True)
        if mode == "fwd":
            o_ref[0, rs, :] = o
            return carry
        o = o + op_ref[0, rs, :]
        nw = nw_ref[...]
        parts = []
        for h in range(HG_HEADS):
            oh = o[:, h * HG_DK:(h + 1) * HG_DK]
            ms = jnp.mean(oh * oh, -1, keepdims=True)
            parts.append(oh * lax.rsqrt(ms + LN_EPS) * nw)
        on = jnp.concatenate(parts, axis=1)
        gs_ref[rs, :] = (on * _silu(og_ref[0, rs, :].astype(F32))).astype(BF16)
        return carry

    lax.fori_loop(0, nchunk, body, 0)

    if mode == "state":
        @pl.when(j == pl.num_programs(1) - 1)
        def _():
            sfin_ref[0] = st_ref[...]
    elif mode == "rev_out":
        y_ref[0] = jnp.dot(gs_ref[...], wa_ref[...], preferred_element_type=F32)


def _tri(n, reverse, block=None):
    block = n if block is None else block
    i = np.arange(n)
    same = (i[:, None] // block) == (i[None, :] // block)
    m = (i[None, :] >= i[:, None]) if reverse else (i[None, :] <= i[:, None])
    return jnp.asarray((m & same).astype(np.float32))


def _hgrn(proj, s0, lb, reverse, mode, tt, o_prev=None, norm_w=None, w_a=None):
    b, t, _ = proj.shape
    nt = t // tt
    hw = HG_WIDTH

    def tmap(col):
        if reverse:
            return lambda bi, j: (bi, nt - 1 - j, col)
        return lambda bi, j: (bi, j, col)

    pspec = lambda col: pl.BlockSpec((1, tt, hw), tmap(col))
    sspec = pl.BlockSpec((1, HG_HEADS, HG_DK, HG_DK), lambda bi, j: (bi, 0, 0, 0))
    cspec = lambda shape: pl.BlockSpec(shape, lambda bi, j: (0,) * len(shape))
    fcol = PB_FB if reverse else PB_FF
    tri = _tri(CHUNK, reverse)
    kern = functools.partial(_hgrn_kernel, reverse=reverse, mode=mode, nchunk=tt // CHUNK)
    st_scr = pltpu.VMEM((HG_HEADS, HG_DK, HG_DK), F32)
    if mode == "state":
        return pl.pallas_call(
            kern, grid=(b, nt),
            in_specs=[pspec(fcol), pspec(PB_IV), sspec, cspec((1, hw)), cspec((CHUNK, CHUNK))],
            out_specs=sspec,
            out_shape=jax.ShapeDtypeStruct(s0.shape, F32),
            scratch_shapes=[st_scr],
            compiler_params=_cparams(("arbitrary", "arbitrary")),
            name="hgrn_state",
        )(proj, proj, s0, lb, tri)
    if mode == "fwd":
        return pl.pallas_call(
            kern, grid=(b, nt),
            in_specs=[pspec(fcol), pspec(PB_IV), pspec(PB_Q), sspec, cspec((1, hw)), cspec((CHUNK, CHUNK))],
            out_specs=pl.BlockSpec((1, tt, hw), tmap(0)),
            out_shape=jax.ShapeDtypeStruct((b, t, hw), F32),
            scratch_shapes=[st_scr],
            compiler_params=_cparams(("arbitrary", "arbitrary")),
            name="hgrn_fwd",
        )(proj, proj, proj, s0, lb, tri)
    return pl.pallas_call(
        kern, grid=(b, nt),
        in_specs=[pspec(fcol), pspec(PB_IV), pspec(PB_Q), pspec(PB_OG),
                  pl.BlockSpec((1, tt, hw), tmap(0)), sspec, cspec((1, hw)), cspec((CHUNK, CHUNK)),
                  cspec((1, HG_DK)), cspec((hw, D_MODEL))],
        out_specs=pl.BlockSpec((1, tt, D_MODEL), tmap(0)),
        out_shape=jax.ShapeDtypeStruct((b, t, D_MODEL), F32),
        scratch_shapes=[st_scr, pltpu.VMEM((tt, hw), BF16)],
        compiler_params=_cparams(("arbitrary", "arbitrary")),
        name="hgrn_rev",
    )(proj, proj, proj, proj, o_prev, s0, lb, tri, norm_w, w_a)


def _conv_silu(xp, w, b):
    r = xp.shape[0]
    rio = lax.broadcasted_iota(I32, (r, 1), 0)
    acc = xp * w[MA_CONV // 2:MA_CONV // 2 + 1] + b
    for kk in range(MA_CONV):
        d = kk - MA_CONV // 2
        if d == 0:
            continue
        sh = pltpu.roll(xp, (-d) % r, axis=0)
        valid = jnp.logical_and(rio + d >= 0, rio + d < r)
        acc = acc + jnp.where(valid, sh, 0.0) * w[kk:kk + 1]
    return _silu(acc)


def _expand(x, e):
    hi = x.astype(BF16)
    lo = (x - hi.astype(F32)).astype(BF16)
    return (jnp.dot(hi, e, preferred_element_type=F32) + jnp.dot(lo, e, preferred_element_type=F32))


def _ssd_kernel(*refs, reverse, mode, nrows):
    if mode == "state":
        (x0_ref, x1_ref, b_ref, c_ref, dt_ref, h0_ref, cw_ref, cb_ref, dtb_ref, a_ref, e_ref,
         tri_ref, hfin_ref, hst_ref) = refs
    elif mode == "fwd":
        (x0_ref, x1_ref, b_ref, c_ref, dt_ref, h0_ref, cw_ref, cb_ref, dtb_ref, a_ref, e_ref,
         tri_ref, y_ref, hst_ref) = refs
    else:
        (x0_ref, x1_ref, b_ref, c_ref, dt_ref, h0_ref, cw_ref, cb_ref, dtb_ref, a_ref, e_ref,
         tri_ref, z0_ref, z1_ref, yf_ref, dsk_ref, nw_ref, wb_ref, y_ref, hst_ref, ys_ref) = refs
    j = pl.program_id(1)

    @pl.when(j == 0)
    def _():
        hst_ref[...] = h0_ref[0]

    cw = cw_ref[...]
    cb = cb_ref[...]
    need_y = mode != "state"
    pieces = []
    for pi, pref in enumerate((x0_ref, x1_ref, b_ref, c_ref)):
        if pi == 3 and not need_y:
            pieces.append(None)
            continue
        cs = slice(pi * 1024, (pi + 1) * 1024)
        pieces.append(_conv_silu(pref[0].astype(F32), cw[:, cs], cb[:, cs]))
    xh = jnp.concatenate(pieces[:2], axis=1)
    bm = pieces[2].astype(BF16)
    cm = pieces[3].astype(BF16) if need_y else None

    x = dt_ref[0] + dtb_ref[...]
    dt = jnp.maximum(x, 0.0) + jnp.log1p(jnp.exp(-jnp.abs(x)))
    la = dt * a_ref[...]
    e_mat = e_ref[...]
    tri = tri_ref[...]
    lane0 = MA_HEADS if reverse else 0
    rows = lax.broadcasted_iota(I32, (CHUNK, CHUNK), 0)
    cols = lax.broadcasted_iota(I32, (CHUNK, CHUNK), 1)
    mask = (cols >= rows) if reverse else (cols <= rows)
    lane128 = lax.broadcasted_iota(I32, (CHUNK, 128), 1)

    npair = nrows // 128
    for pbi in range(npair):
        pb = npair - 1 - pbi if reverse else pbi
        prow = slice(pb * 128, (pb + 1) * 128)
        lcum2 = jnp.dot(tri, la[prow], precision=HI, preferred_element_type=F32)
        lcum_t = lcum2.T
        dt_t = dt[prow].T
        for ci in range(2):
            cc = 1 - ci if reverse else ci
            crow = slice(pb * 128 + cc * CHUNK, pb * 128 + (cc + 1) * CHUNK)
            ccol = slice(cc * CHUNK, (cc + 1) * CHUNK)
            lc = lcum2[ccol]
            e = 0 if reverse else CHUNK - 1
            llast = lc[e:e + 1]
            xh_c = xh[crow]
            xb_c = xh_c.astype(BF16)
            w_out = _expand(dt[crow] * jnp.exp(llast - lc), e_mat)
            xd = (xh_c * w_out).astype(BF16)
            expl = _expand(jnp.broadcast_to(jnp.exp(llast), (8, 128)), e_mat)[0:1]
            bc = bm[crow]
            if need_y:
                dec_in = _expand(jnp.exp(lc), e_mat)
                cc_ = cm[crow]
            ys = []
            for g in range(MA_GROUPS):
                gs = slice(g * MA_STATE, (g + 1) * MA_STATE)
                gw = slice(g * MA_GW, (g + 1) * MA_GW)
                hg = hst_ref[g]
                if need_y:
                    cbg = lax.dot_general(cc_[:, gs], bc[:, gs], NT_DIMS, preferred_element_type=F32)
                    pair_out = []
                    for jp in range(2):
                        ms = []
                        for jj in range(2):
                            ln = lane0 + 4 * g + 2 * jp + jj
                            diff = lc[:, ln:ln + 1] - lcum_t[ln:ln + 1, ccol]
                            lmat = jnp.exp(jnp.where(mask, diff, NEG))
                            ms.append(cbg * lmat * dt_t[ln:ln + 1, ccol])
                        lhs = jnp.concatenate(ms, axis=0).astype(BF16)
                        xs = xb_c[:, g * MA_GW + jp * 128:g * MA_GW + (jp + 1) * 128]
                        res = jnp.dot(lhs, xs, preferred_element_type=F32)
                        pair_out.append(jnp.where(lane128 < MA_HEAD_DIM, res[:CHUNK], res[CHUNK:]))
                    y_intra = jnp.concatenate(pair_out, axis=1)
                    y_inter = jnp.dot(cc_[:, gs], hg.astype(BF16), preferred_element_type=F32)
                    ys.append(y_intra + y_inter * dec_in[:, gw])
                upd = lax.dot_general(bc[:, gs], xd[:, gw], TN_DIMS, preferred_element_type=F32)
                hst_ref[g] = expl[:, gw] * hg + upd
            if not need_y:
                continue
            y = jnp.concatenate(ys, axis=1)
            if mode == "fwd":
                y_ref[0, crow, :] = y
            else:
                ytot = yf_ref[0, crow, :] + y + xh_c * dsk_ref[...]
                zc = jnp.concatenate([z0_ref[0, crow, :], z1_ref[0, crow, :]], axis=1).astype(F32)
                yz = ytot * _silu(zc)
                nw = nw_ref[...]
                parts = []
                for g in range(MA_GROUPS):
                    gw = slice(g * MA_GW, (g + 1) * MA_GW)
                    yg = yz[:, gw]
                    ms_ = jnp.mean(yg * yg, -1, keepdims=True)
                    parts.append(yg * lax.rsqrt(ms_ + LN_EPS) * nw[:, gw])
                ys_ref[crow, :] = jnp.concatenate(parts, axis=1).astype(BF16)

    if mode == "state":
        @pl.when(j == pl.num_programs(1) - 1)
        def _():
            hfin_ref[0] = hst_ref[...]
    elif mode == "rev_out":
        y_ref[0] = jnp.dot(ys_ref[...], wb_ref[...], preferred_element_type=F32)


def _ssd(proj, dtraw, h0, consts, reverse, mode, nrows, ncols, y_f=None, extra=None):
    b, t, _ = proj.shape
    pv = proj.reshape(b, nrows, ncols * NPROJ)
    dv = dtraw.reshape(b, nrows, ncols * DT_W)
    conv_w, conv_b, dt_bias, a128, e_mat = consts

    def wmap(nblk, col):
        if reverse:
            return lambda bi, j: (bi, 0, (ncols - 1 - j) * nblk + col)
        return lambda bi, j: (bi, 0, j * nblk + col)

    pspec = lambda col: pl.BlockSpec((1, nrows, 1024), wmap(NP_BLOCKS, col))
    dspec = pl.BlockSpec((1, nrows, DT_W), wmap(1, 0))
    hspec = pl.BlockSpec((1, MA_GROUPS, MA_STATE, MA_GW), lambda bi, j: (bi, 0, 0, 0))
    cspec = lambda shape: pl.BlockSpec(shape, lambda bi, j: (0,) * len(shape))
    tri = _tri(128, reverse, block=CHUNK)
    kern = functools.partial(_ssd_kernel, reverse=reverse, mode=mode, nrows=nrows)
    base_specs = [pspec(PB_X0), pspec(PB_X1), pspec(PB_B), pspec(PB_C), dspec, hspec,
                  cspec((MA_CONV, 4096)), cspec((1, 4096)), cspec((1, DT_W)), cspec((1, DT_W)),
                  cspec((DT_W, MA_INNER)), cspec((128, 128))]
    base_args = [pv, pv, pv, pv, dv, h0, conv_w, conv_b, dt_bias, a128, e_mat, tri]
    h_scr = pltpu.VMEM((MA_GROUPS, MA_STATE, MA_GW), F32)
    sem = _cparams(("arbitrary", "arbitrary"))
    if mode == "state":
        return pl.pallas_call(
            kern, grid=(b, ncols), in_specs=base_specs, out_specs=hspec,
            out_shape=jax.ShapeDtypeStruct(h0.shape, F32), scratch_shapes=[h_scr],
            compiler_params=sem, name="ssd_state")(*base_args)
    if mode == "fwd":
        out = pl.pallas_call(
            kern, grid=(b, ncols), in_specs=base_specs,
            out_specs=pl.BlockSpec((1, nrows, MA_INNER), wmap(1, 0)),
            out_shape=jax.ShapeDtypeStruct((b, nrows, ncols * MA_INNER), F32),
            scratch_shapes=[h_scr], compiler_params=sem, name="ssd_fwd")(*base_args)
        return out
    d_skip, norm_w, w_b = extra
    out = pl.pallas_call(
        kern, grid=(b, ncols),
        in_specs=base_specs + [pspec(PB_Z0), pspec(PB_Z1),
                               pl.BlockSpec((1, nrows, MA_INNER), wmap(1, 0)),
                               cspec((1, MA_INNER)), cspec((1, MA_INNER)), cspec((MA_INNER, D_MODEL))],
        out_specs=pl.BlockSpec((1, nrows, D_MODEL), wmap(1, 0)),
        out_shape=jax.ShapeDtypeStruct((b, nrows, ncols * D_MODEL), F32),
        scratch_shapes=[h_scr, pltpu.VMEM((nrows, MA_INNER), BF16)],
        compiler_params=sem, name="ssd_rev")(*base_args, pv, pv, y_f, d_skip, norm_w, w_b)
    return out.reshape(b, t, D_MODEL)


def _merge_kernel(x_ref, ya_ref, yb_ref, ga_ref, gb_ref, g1_ref, wo_ref, lg_ref, lb_ref, o_ref):
    m = (_sigmoid(ga_ref[...].astype(F32)) * ya_ref[...]
         + _sigmoid(gb_ref[...].astype(F32)) * yb_ref[...])
    out = jnp.dot(m.astype(BF16), wo_ref[...], preferred_element_type=F32)
    z = DN_ALPHA * x_ref[...] + g1_ref[0] * out
    o_ref[...] = _layer_norm(z) * lg_ref[...] + lb_ref[...]


def _merge(x2, y_a, y_b, proj2, g1, w_out, ln_g, ln_b, tokens_per_batch, tm):
    m, d = x2.shape
    tpb = tokens_per_batch // tm
    tspec = pl.BlockSpec((tm, d), lambda i: (i, 0))
    cspec = lambda shape: pl.BlockSpec(shape, lambda i: (0,) * len(shape))
    return pl.pallas_call(
        _merge_kernel, grid=(m // tm,),
        in_specs=[tspec, tspec, tspec,
                  pl.BlockSpec((tm, d), lambda i: (i, PB_GA)),
                  pl.BlockSpec((tm, d), lambda i: (i, PB_GB)),
                  pl.BlockSpec((1, 1, d), lambda i: (i // tpb, 0, 0)),
                  cspec((d, d)), cspec((1, d)), cspec((1, d))],
        out_specs=tspec,
        out_shape=jax.ShapeDtypeStruct((m, d), F32),
        compiler_params=_cparams(("arbitrary",)),
        name="merge",
    )(x2, y_a, y_b, proj2, proj2, g1, w_out, ln_g, ln_b)


KEY_BIAS = 0x40000000
KEY_MIN = 1.1754944e-38
LOW_MASK = PEER_NKEYS - 1


def _sort_pairs(n):
    def merge(lo, hi, r):
        step = r * 2
        if step < hi - lo:
            yield from merge(lo, hi, step)
            yield from merge(lo + r, hi, step)
            for i in range(lo + r, hi - r, step):
                yield (i, i + r)
        else:
            yield (lo, lo + r)

    def sort(lo, hi):
        if hi - lo >= 1:
            mid = lo + (hi - lo) // 2
            yield from sort(lo, mid)
            yield from sort(mid + 1, hi)
            yield from merge(lo, hi, 1)

    return list(sort(0, n - 1))


def _merge_pairs(n):
    out, j = [], n // 2
    while j:
        out += [(i, i | j) for i in range(n) if not i & j]
        j //= 2
    return out


def _cex(v, pairs):
    v = list(v)
    for i, j in pairs:
        v[i], v[j] = jnp.maximum(v[i], v[j]), jnp.minimum(v[i], v[j])
    return v


def _cex_p(v, c, pairs):
    v, c = list(v), list(c)
    for i, j in pairs:
        m = v[i] >= v[j]
        v[i], v[j], c[i], c[j] = (jnp.maximum(v[i], v[j]), jnp.minimum(v[i], v[j]),
                                  jnp.where(m, c[i], c[j]), jnp.where(m, c[j], c[i]))
    return v, c


def _to_key(s, low):
    b = pltpu.bitcast(s, I32)
    k = jnp.where(b < 0, jnp.bitwise_xor(b, 0x7FFFFFFF), b)
    k = jnp.right_shift(k, 1) + KEY_BIAS
    k = jnp.bitwise_or(jnp.bitwise_and(k, ~LOW_MASK), low)
    return pltpu.bitcast(k, F32)


def _from_key(kf):
    k = pltpu.bitcast(kf, I32)
    low = jnp.bitwise_and(k, LOW_MASK)
    k = jnp.left_shift(jnp.bitwise_and(k, ~LOW_MASK) - KEY_BIAS, 1)
    b = jnp.where(k < 0, jnp.bitwise_xor(k, 0x7FFFFFFF), k)
    return pltpu.bitcast(b, F32), low


def _top16_keys(st, sub):
    v = []
    for r in range(PEER_NKEYS // 8):
        v.append(_to_key(st[8 * r:8 * r + 8], (LOW_MASK - 8 * r) - sub))
    v = _cex(v, _sort_pairs(16))
    for sh in (4, 2, 1):
        part = [pltpu.roll(x, sh, axis=0) for x in v]
        v = _cex([jnp.maximum(v[r], part[15 - r]) for r in range(16)], _merge_pairs(16))
    return v


def _natural(lst, half, sub):
    x = lst[8 * half]
    for s_ in range(1, 8):
        x = jnp.where(sub == s_, lst[8 * half + s_], x)
    return x


def _slot_table():
    slots = {}
    for s_ in range(8):
        slots[(0, s_)] = (0, s_)
        slots[(1, s_)] = (0, 8 + s_)
        slots[(3, s_)] = (8 + s_, 0)
    for s_ in range(1, 8):
        slots[(2, s_)] = (s_, 0)
        slots[(4, s_)] = (1, s_)
    for s_ in range(2, 8):
        slots[(5, s_)] = (s_, 1)
    for s_ in (2, 3, 4):
        slots[(6, s_)] = (2, s_)
    for s_ in (2, 3):
        slots[(7, s_)] = (3, s_)
    slots[(7, 4)] = (4, 2)
    want = sorted((j, l) for j in range(PEER_TOPK) for l in range(PEER_TOPK) if (j + 1) * (l + 1) <= PEER_TOPK)
    assert sorted(slots.values()) == want
    low = -np.ones((8, 8), np.int32)
    for (q, s_), jl in slots.items():
        low[q, s_] = LOW_MASK - want.index(jl)
    return jnp.asarray(np.repeat(low.reshape(64, 1), 128, axis=1))


def _retrieve_head(q_ref, sk_ref, slot_low, sub, t0, h):
    vals, idxs = [], []
    for i in range(2):
        cidx = 2 * h + i
        qc = q_ref[pl.ds(t0, 128), cidx * PEER_HALF:(cidx + 1) * PEER_HALF]
        st = lax.dot_general(sk_ref[cidx], qc, NT_DIMS, preferred_element_type=F32)
        dec = [_from_key(k) for k in _top16_keys(st, sub)]
        vals.append([d[0] for d in dec])
        idxs.append([LOW_MASK - d[1] for d in dec])
    v1, v2 = vals
    i1, i2 = idxs
    s1n = [_natural(v1, hf, sub) for hf in range(2)]
    s2n = [_natural(v2, hf, sub) for hf in range(2)]
    i1n = [_natural(i1, hf, sub) * PEER_NKEYS for hf in range(2)]
    i2n = [_natural(i2, hf, sub) for hf in range(2)]
    row4 = sub == 4
    cand = [(v1[0] + s2n[0], i1[0] * PEER_NKEYS + i2n[0]),
            (v1[0] + s2n[1], i1[0] * PEER_NKEYS + i2n[1]),
            (s1n[0] + v2[0], i1n[0] + i2[0]),
            (s1n[1] + v2[0], i1n[1] + i2[0]),
            (v1[1] + s2n[0], i1[1] * PEER_NKEYS + i2n[0]),
            (s1n[0] + v2[1], i1n[0] + i2[1]),
            (v1[2] + s2n[0], i1[2] * PEER_NKEYS + i2n[0]),
            (jnp.where(row4, s1n[0] + v2[2], v1[3] + s2n[0]),
             jnp.where(row4, i1n[0] + i2[2], i1[3] * PEER_NKEYS + i2n[0]))]
    keys, codes = [], []
    for qi, (cv, cc) in enumerate(cand):
        low = slot_low[8 * qi:8 * qi + 8]
        keys.append(jnp.where(low >= 0, _to_key(cv, low), KEY_MIN))
        codes.append(cc)
    keys, codes = _cex_p(keys, codes, _sort_pairs(8))
    pk = [pltpu.roll(x, 4, axis=0) for x in keys]
    pc = [pltpu.roll(x, 4, axis=0) for x in codes]
    keys, codes = _cex_p(keys + pk[::-1], codes + pc[::-1], _merge_pairs(16))
    for sh in (2, 1):
        pk = [pltpu.roll(x, sh, axis=0) for x in keys]
        pc = [pltpu.roll(x, sh, axis=0) for x in codes]
        mk, mc = [], []
        for r in range(16):
            m = keys[r] >= pk[15 - r]
            mk.append(jnp.maximum(keys[r], pk[15 - r]))
            mc.append(jnp.where(m, codes[r], pc[15 - r]))
        keys, codes = _cex_p(mk, mc, _merge_pairs(16))
    tv = [_from_key(k)[0] for k in keys]
    ex = [jnp.exp(t - tv[0]) for t in tv]
    den = ex[0]
    for e_ in ex[1:]:
        den = den + e_
    inv = 1.0 / den
    return codes, [e_ * inv for e_ in ex]


def _peer_kernel(x_ref, sh_ref, sc_ref, g2_ref, wq_ref, sk_ref, slot_ref, u_ref, v_ref, lg_ref, lb_ref,
                 o_ref, xm_ref, q_ref, acc_ref, t_ref, tc_ref, tg_ref, ai_ref, bi_ref, gt_ref):
    s = pl.program_id(1)
    p = x_ref.shape[0]

    @pl.when(s == 0)
    def _():
        xm = (_layer_norm(x_ref[...]) * (1.0 + sc_ref[0]) + sh_ref[0]).astype(BF16)
        xm_ref[...] = xm
        acc_ref[...] = jnp.zeros_like(acc_ref)
        q_ref[...] = jnp.dot(xm, wq_ref[...], preferred_element_type=F32).astype(BF16)
        slot_low = slot_ref[...]
        sub = lax.broadcasted_iota(I32, (8, 128), 0)

        def tile(ti, carry):
            t0 = pl.multiple_of(ti * 128, 128)
            for h in range(PEER_HEADS):
                codes, gates = _retrieve_head(q_ref, sk_ref, slot_low, sub, t0, h)
                for r in range(PEER_TOPK):
                    row = h * PEER_TOPK + r
                    tc_ref[row:row + 1, :] = codes[r][0:1].astype(F32)
                    tg_ref[row:row + 1, :] = gates[r][0:1]
            ct = tc_ref[...].T.astype(I32)
            ai_ref[pl.ds(t0, 128), :] = jnp.right_shift(ct, 7).astype(F32)
            bi_ref[pl.ds(t0, 128), :] = jnp.bitwise_and(ct, LOW_MASK).astype(F32)
            gt_ref[pl.ds(t0, 128), :] = tg_ref[...].T
            return carry

        lax.fori_loop(0, p // 128, tile, 0)

        sub128 = lax.broadcasted_iota(I32, (PEER_NKEYS, 128), 0)
        amap = jnp.bitwise_or(jnp.left_shift(jnp.bitwise_and(sub128, 63), 1),
                              jnp.right_shift(sub128, 6)).astype(F32)
        bmap = sub128.astype(F32)

        def tok8(i, carry):
            t0 = pl.multiple_of(i * 8, 8)
            a8 = ai_ref[pl.ds(t0, 8), :]
            b8 = bi_ref[pl.ds(t0, 8), :]
            g8 = gt_ref[pl.ds(t0, 8), :]
            for j in range(8):
                at = jnp.where(a8[j:j + 1] == amap, 1.0, 0.0).astype(BF16)
                rt = jnp.where(b8[j:j + 1] == bmap, g8[j:j + 1], 0.0).astype(BF16)
                w = lax.dot_general(at, rt, NT_DIMS, preferred_element_type=F32)
                lo = pltpu.bitcast(w[:64], I32) + 0x8000
                hi = pltpu.bitcast(w[64:], I32) + 0x8000
                lo = lax.shift_right_logical(lo, 16)
                hi = jnp.bitwise_and(hi, -65536)
                r0 = pl.multiple_of((t0 + j) * PEER_PITCH, 8)
                t_ref[pl.ds(r0, 64), :] = jnp.bitwise_or(lo, hi)
            return carry

        lax.fori_loop(0, p // 8, tok8, 0)

    xm = xm_ref[...]
    tot = None
    for g in range(PEER_G):
        a = s * PEER_G + g
        wp = t_ref[pl.ds(a, p, stride=PEER_PITCH), :]
        wlo = pltpu.bitcast(jnp.left_shift(wp, 16), F32)
        whi = pltpu.bitcast(jnp.bitwise_and(wp, -65536), F32)
        wgt = jnp.concatenate([wlo, whi], axis=1)
        rows = slice(g * 2 * PEER_NKEYS, (g + 1) * 2 * PEER_NKEYS)
        hval = lax.dot_general(xm, u_ref[rows, :], NT_DIMS, preferred_element_type=F32)
        act = 0.5 * hval * (1.0 + lax.erf(hval * 0.7071067811865476))
        z = (wgt * act).astype(BF16)
        d = jnp.dot(z, v_ref[rows, :], preferred_element_type=F32)
        tot = d if tot is None else tot + d
    acc_ref[...] += tot

    @pl.when(s == pl.num_programs(1) - 1)
    def _():
        z = DN_ALPHA * x_ref[...] + g2_ref[0] * acc_ref[...]
        o_ref[...] = _layer_norm(z) * lg_ref[...] + lb_ref[...]


def _peer(x1, shift, scale, g2, wq, sk, u2, v2, ln_g, ln_b, tokens_per_batch):
    m, d = x1.shape
    p = PEER_P
    tpb = tokens_per_batch // p
    blk = PEER_G * 2 * PEER_NKEYS
    nsteps = u2.shape[0] // blk
    slot = _slot_table()
    cspec = lambda shape: pl.BlockSpec(shape, lambda i, s: (0,) * len(shape))
    bspec = pl.BlockSpec((1, 1, d), lambda i, s: (i // tpb, 0, 0))
    nent = PEER_HEADS * PEER_TOPK
    return pl.pallas_call(
        _peer_kernel, grid=(m // p, nsteps),
        in_specs=[pl.BlockSpec((p, d), lambda i, s: (i, 0)), bspec, bspec, bspec,
                  cspec(wq.shape), cspec(sk.shape), cspec(slot.shape),
                  pl.BlockSpec((blk, d), lambda i, s: (s, 0)),
                  pl.BlockSpec((blk, d), lambda i, s: (s, 0)),
                  cspec((1, d)), cspec((1, d))],
        out_specs=pl.BlockSpec((p, d), lambda i, s: (i, 0)),
        out_shape=jax.ShapeDtypeStruct((m, d), F32),
        scratch_shapes=[pltpu.VMEM((p, d), BF16), pltpu.VMEM((p, 2 * PEER_HEADS * PEER_HALF), BF16),
                        pltpu.VMEM((p, d), F32), pltpu.VMEM((p * PEER_PITCH, 128), I32),
                        pltpu.VMEM((nent, 128), F32), pltpu.VMEM((nent, 128), F32),
                        pltpu.VMEM((p, nent), F32), pltpu.VMEM((p, nent), F32), pltpu.VMEM((p, nent), F32)],
        compiler_params=_cparams(("arbitrary", "arbitrary")),
        name="peer",
    )(x1, shift, scale, g2, wq, sk, slot, u2, v2, ln_g, ln_b)


def _layer(x, c, ctx, c_ctx, lw, lb_l):
    b, t, d = x.shape
    tc = ctx.shape[1]
    rows = t // GRID_W

    npad = -(-(b + 1) // 8) * 8
    c_all = jnp.zeros((npad, d), F32).at[:b].set(c).at[b].set(c_ctx)
    mod = _ada(c_all, lw["w_ada"], lw["b_ada"])
    sh1, sc1, g1, sh2, sc2, g2 = [mod[:b, i * d:(i + 1) * d].reshape(b, 1, d) for i in range(6)]
    csh1 = jnp.broadcast_to(mod[b, 0:d], (b, 1, d))
    csc1 = jnp.broadcast_to(mod[b, d:2 * d], (b, 1, d))

    w_in = lw["w_in"]
    off = np.cumsum([0, HG_WIDTH, HG_WIDTH, HG_WIDTH, MA_INNER + MA_GROUPS * MA_STATE, MA_HEADS, MA_HEADS,
                     HG_WIDTH, HG_WIDTH, MA_GROUPS * MA_STATE, MA_INNER, D_MODEL, D_MODEL])
    o_ff, o_fb, o_iv, o_xb, o_dtf, o_dtb, o_q, o_og, o_cm, o_z, o_ga, o_gb = off[:12]
    col = lambda o, n: w_in[:, o:o + n]
    w_main = jnp.concatenate([col(o_ff, 1024), col(o_fb, 1024), col(o_iv, 1024), col(o_q, 1024),
                              col(o_xb, 3072), col(o_cm, 1024), col(o_z, 2048), col(o_og, 1024),
                              col(o_ga, 1024), col(o_gb, 1024)], axis=1).astype(BF16)
    w_dt = jnp.concatenate([col(o_dtf, 2 * MA_HEADS), jnp.zeros((d, DT_W - 2 * MA_HEADS), F32)],
                           axis=1).astype(BF16)
    lb_f = lb_l[0].reshape(1, HG_WIDTH)
    lb_b = lb_l[1].reshape(1, HG_WIDTH)
    zpad = jnp.zeros((DT_W - 2 * MA_HEADS,), F32)
    dt_bias = jnp.concatenate([lw["ma_dt_bias"].reshape(-1), zpad]).reshape(1, DT_W)
    a128 = jnp.concatenate([-jnp.exp(lw["ma_a_log"].astype(F32)).reshape(-1), zpad]).reshape(1, DT_W)
    conv_w = lw["ma_conv_w"].T
    conv_b = lw["ma_conv_b"].reshape(1, -1)
    head_of = np.arange(MA_INNER) // MA_HEAD_DIM
    e_f = jnp.asarray((np.arange(DT_W)[:, None] == head_of[None, :]).astype(np.float32)).astype(BF16)
    e_b = jnp.asarray((np.arange(DT_W)[:, None] == (head_of + MA_HEADS)[None, :]).astype(np.float32)).astype(BF16)
    consts_f = (conv_w, conv_b, dt_bias, a128, e_f)
    consts_b = (conv_w, conv_b, dt_bias, a128, e_b)
    d_skip = jnp.repeat(lw["ma_d"], MA_HEAD_DIM).reshape(1, MA_INNER)
    ma_norm_w = lw["ma_norm_w"].reshape(1, MA_INNER)
    hg_norm_w = lw["hg_norm_w"].reshape(1, HG_DK)
    w_a = lw["w_branch_a"].astype(BF16)
    w_b = lw["w_branch_b"].astype(BF16)
    w_o = lw["w_out"].astype(BF16)

    projc, dtc = _inproj(ctx.reshape(b * tc, d), csh1, csc1, w_main, w_dt, tc, tc)
    projc = projc.reshape(b, tc, NPROJ)
    dtc = dtc.reshape(b, tc, DT_W)
    zs = jnp.zeros((b, HG_HEADS, HG_DK, HG_DK), F32)
    zh = jnp.zeros((b, MA_GROUPS, MA_STATE, MA_GW), F32)
    s_hf = _hgrn(projc, zs, lb_f, False, "state", tc)
    s_hb = _hgrn(projc, zs, lb_b, True, "state", tc)
    h_f = _ssd(projc, dtc, zh, consts_f, False, "state", tc, 1)
    h_b = _ssd(projc, dtc, zh, consts_b, True, "state", tc, 1)

    x2 = x.reshape(b * t, d)
    tm = min(1024, t)
    proj, dtr = _inproj(x2, sh1, sc1, w_main, w_dt, t, tm)
    proj3 = proj.reshape(b, t, NPROJ)
    dtr = dtr.reshape(b, t, DT_W)
    tt = min(512, t)
    o_f = _hgrn(proj3, s_hf, lb_f, False, "fwd", tt)
    y_a = _hgrn(proj3, s_hb, lb_b, True, "rev_out", tt, o_prev=o_f, norm_w=hg_norm_w, w_a=w_a)
    y_f = _ssd(proj3, dtr, h_f, consts_f, False, "fwd", rows, GRID_W)
    y_b = _ssd(proj3, dtr, h_b, consts_b, True, "rev_out", rows, GRID_W, y_f=y_f,
               extra=(d_skip, ma_norm_w, w_b))
    x1 = _merge(x2, y_a.reshape(b * t, d), y_b.reshape(b * t, d), proj, g1, w_o,
                lw["ln1_g"].reshape(1, d), lw["ln1_b"].reshape(1, d), t, min(512, t))

    wq = lw["peer_wq"].astype(BF16)
    sk = lw["peer_subkeys"].reshape(PEER_HEADS * 2, PEER_NKEYS, PEER_HALF).astype(BF16)
    u2 = lw["peer_u"].astype(BF16)
    v2 = lw["peer_v"].astype(BF16)
    out = _peer(x1, sh2, sc2, g2, wq, sk, u2, v2, lw["ln2_g"].reshape(1, d), lw["ln2_b"].reshape(1, d), t)
    return out.reshape(b, t, d)


def kernel(x, c, ctx, c_ctx, w_ada, b_ada, w_in, hg_lb_logits, hg_norm_w, ma_conv_w, ma_conv_b, ma_dt_bias,
           ma_a_log, ma_d, ma_norm_w, w_branch_a, w_branch_b, w_out, ln1_g, ln1_b, peer_wq, peer_subkeys,
           peer_u, peer_v, ln2_g, ln2_b):
    depth = w_in.shape[0]
    assert depth == 1, "the scan states of a single (last) layer are implemented"
    lb_all = jnp.cumsum(jax.nn.softmax(hg_lb_logits.astype(F32), axis=1), axis=1)
    lw = {"w_ada": w_ada[0], "b_ada": b_ada[0], "w_in": w_in[0], "hg_norm_w": hg_norm_w[0],
          "ma_conv_w": ma_conv_w[0], "ma_conv_b": ma_conv_b[0], "ma_dt_bias": ma_dt_bias[0],
          "ma_a_log": ma_a_log[0], "ma_d": ma_d[0], "ma_norm_w": ma_norm_w[0],
          "w_branch_a": w_branch_a[0], "w_branch_b": w_branch_b[0], "w_out": w_out[0],
          "ln1_g": ln1_g[0], "ln1_b": ln1_b[0], "peer_wq": peer_wq[0], "peer_subkeys": peer_subkeys[0],
          "peer_u": peer_u[0], "peer_v": peer_v[0], "ln2_g": ln2_g[0], "ln2_b": ln2_b[0]}
    return _layer(x, c, ctx, c_ctx, lw, lb_all[:, 0])
```

```python
import functools

import numpy as np
import jax
import jax.numpy as jnp
from jax import lax
from jax.experimental import pallas as pl
from jax.experimental.pallas import tpu as pltpu

F32 = jnp.float32
BF16 = jnp.bfloat16
I32 = jnp.int32
HI = lax.Precision.HIGHEST

D_MODEL = 1024
GRID_W = 64
CHUNK = 64
SUB = 16
HG_HEADS = 8
HG_DK = 128
HG_WIDTH = HG_HEADS * HG_DK
MA_INNER = 2 * D_MODEL
MA_HEAD_DIM = 64
MA_HEADS = MA_INNER // MA_HEAD_DIM
MA_GROUPS = 8
MA_STATE = 128
MA_GW = MA_INNER // MA_GROUPS
MA_CONV = 5
PEER_HEADS = 8
PEER_NKEYS = 128
PEER_TOPK = 16
PEER_HALF = 128
DN_ALPHA = 2.0 ** 0.25
LN_EPS = 1e-6
NEG = -3.0e38
EXP_CLAMP = 60.0

PB_FF, PB_FB, PB_IV, PB_Q, PB_OG, PB_GA, PB_GB = range(7)
PC_X0, PC_X1, PC_B, PC_C, PC_Z0, PC_Z1 = range(6)
NP_RM = 7 * 1024
NP_CM = 6 * 1024
DT_W = 128
COL_TILE = 8

PEER_P = 512
PEER_G = 4
PEER_PITCH = 72
TOK_UNROLL = 16
VMEM_LIMIT = 56 * 1024 * 1024

NT_DIMS = (((1,), (1,)), ((), ()))
TN_DIMS = (((0,), (0,)), ((), ()))


def _cparams(sem):
    return pltpu.CompilerParams(dimension_semantics=sem, vmem_limit_bytes=VMEM_LIMIT)


def _sigmoid(x):
    return jax.nn.sigmoid(x)


def _silu(x):
    return x * jax.nn.sigmoid(x)


def _layer_norm(x):
    mu = jnp.mean(x, -1, keepdims=True)
    xc = x - mu
    var = jnp.mean(xc * xc, -1, keepdims=True)
    return xc * lax.rsqrt(var + LN_EPS)


def _ada_kernel(c_ref, w_ref, b_ref, o_ref):
    c = c_ref[...]
    o_ref[...] = jnp.dot(_silu(c), w_ref[...], precision=HI, preferred_element_type=F32) + b_ref[...]


def _ada(c_all, w_ada, b_ada):
    n, d = c_all.shape
    e = w_ada.shape[1]
    tn = 1024
    return pl.pallas_call(
        _ada_kernel,
        grid=(e // tn,),
        in_specs=[pl.BlockSpec((n, d), lambda j: (0, 0)),
                  pl.BlockSpec((d, tn), lambda j: (0, j)),
                  pl.BlockSpec((1, tn), lambda j: (0, j))],
        out_specs=pl.BlockSpec((n, tn), lambda j: (0, j)),
        out_shape=jax.ShapeDtypeStruct((n, e), F32),
        compiler_params=_cparams(("arbitrary",)),
        name="ada",
    )(c_all, w_ada, b_ada.reshape(1, e))


def _inproj_rm_kernel(x_ref, sh_ref, sc_ref, w_ref, o_ref, u_ref):
    @pl.when(pl.program_id(1) == 0)
    def _():
        u = _layer_norm(x_ref[...]) * (1.0 + sc_ref[0]) + sh_ref[0]
        u_ref[...] = u.astype(BF16)

    o_ref[...] = jnp.dot(u_ref[...], w_ref[...], preferred_element_type=F32).astype(BF16)


def _inproj_rm(x2, shift, scale, w_rm, tokens_per_batch, tm):
    m, d = x2.shape
    tpb = tokens_per_batch // tm
    tn = 1024
    return pl.pallas_call(
        _inproj_rm_kernel,
        grid=(m // tm, NP_RM // tn),
        in_specs=[pl.BlockSpec((tm, d), lambda i, j: (i, 0)),
                  pl.BlockSpec((1, 1, d), lambda i, j: (i // tpb, 0, 0)),
                  pl.BlockSpec((1, 1, d), lambda i, j: (i // tpb, 0, 0)),
                  pl.BlockSpec((d, tn), lambda i, j: (0, j))],
        out_specs=pl.BlockSpec((tm, tn), lambda i, j: (i, j)),
        out_shape=jax.ShapeDtypeStruct((m, NP_RM), BF16),
        scratch_shapes=[pltpu.VMEM((tm, d), BF16)],
        compiler_params=_cparams(("arbitrary", "arbitrary")),
        name="inproj_rm",
    )(x2, shift, scale, w_rm)


def _inproj_cm_kernel(*refs, permute):
    if permute:
        x_ref, sh_ref, sc_ref, perm_ref, w_ref, wdt_ref, o_ref, dt_ref, u_ref = refs
    else:
        x_ref, sh_ref, sc_ref, w_ref, wdt_ref, o_ref, dt_ref, u_ref = refs
    tm, d = u_ref.shape

    @pl.when(pl.program_id(2) == 0)
    def _():
        x = x_ref[...].reshape(tm, d)
        ub = (_layer_norm(x) * (1.0 + sc_ref[0]) + sh_ref[0]).astype(BF16)
        if permute:
            ub = jnp.dot(perm_ref[...], ub, preferred_element_type=F32).astype(BF16)
        u_ref[...] = ub
        dt_ref[...] = jnp.dot(ub, wdt_ref[...], preferred_element_type=F32).reshape(dt_ref.shape)

    o_ref[...] = jnp.dot(u_ref[...], w_ref[...], preferred_element_type=F32).astype(BF16).reshape(o_ref.shape)


def _perm_matrix(nrows, ncols):
    n = nrows * ncols
    p = np.zeros((n, n), np.float32)
    r, w = np.meshgrid(np.arange(nrows), np.arange(ncols), indexing="ij")
    p[(w * nrows + r).ravel(), (r * ncols + w).ravel()] = 1.0
    return p


def _inproj_cm(x, shift, scale, w_cm, w_dt, ncols):
    b, t, d = x.shape
    rows = t // ncols
    tn = 1024
    cspec = lambda shape: pl.BlockSpec(shape, lambda bi, wi, j: (0,) * len(shape))
    mspec = pl.BlockSpec((1, 1, d), lambda bi, wi, j: (bi, 0, 0))
    if ncols == 1:
        ct, tm = 1, rows
        xv = x
        xspec = pl.BlockSpec((1, rows, d), lambda bi, wi, j: (bi, 0, 0))
        extra, especs = [], []
    else:
        ct, tm = COL_TILE, rows * COL_TILE
        xv = x.reshape(b, rows, ncols, d)
        xspec = pl.BlockSpec((1, rows, ct, d), lambda bi, wi, j: (bi, 0, wi, 0))
        extra = [jnp.asarray(_perm_matrix(rows, ct)).astype(BF16)]
        especs = [cspec((tm, tm))]
    return pl.pallas_call(
        functools.partial(_inproj_cm_kernel, permute=ncols > 1),
        grid=(b, ncols // ct, NP_CM // tn),
        in_specs=[xspec, mspec, mspec] + especs +
                 [pl.BlockSpec((d, tn), lambda bi, wi, j: (0, j)), cspec((d, DT_W))],
        out_specs=[pl.BlockSpec((1, ct, rows, tn), lambda bi, wi, j: (bi, wi, 0, j)),
                   pl.BlockSpec((1, ct, rows, DT_W), lambda bi, wi, j: (bi, wi, 0, 0))],
        out_shape=[jax.ShapeDtypeStruct((b, ncols, rows, NP_CM), BF16),
                   jax.ShapeDtypeStruct((b, ncols, rows, DT_W), F32)],
        scratch_shapes=[pltpu.VMEM((tm, d), BF16)],
        compiler_params=_cparams(("arbitrary", "arbitrary", "arbitrary")),
        name="inproj_cm",
    )(xv, shift, scale, *extra, w_cm, w_dt)


def _hgrn_chunk(fraw, v, q, lb, tri, st_ref, reverse, need_o):
    c = fraw.shape[0]
    fg = lb + (1.0 - lb) * _sigmoid(fraw)
    logf = jnp.log(fg)
    k = 1.0 - fg
    cum = jnp.dot(tri, logf, precision=HI, preferred_element_type=F32)
    e = 0 if reverse else c - 1
    glast = cum[e:e + 1]
    kst = (k * jnp.exp(glast - cum)).astype(BF16)
    vb = v.astype(BF16)
    dec = jnp.exp(glast)
    if need_o:
        qin = (q * jnp.exp(cum)).astype(BF16)
        rows = lax.broadcasted_iota(I32, (c, c), 0)
        cols = lax.broadcasted_iota(I32, (c, c), 1)
        mask = (cols >= rows) if reverse else (cols <= rows)
        qhat, khat = [], []
        for i in range(c // SUB):
            ei = i * SUB + SUB - 1 if reverse else i * SUB
            ref = cum[ei:ei + 1] - logf[ei:ei + 1]
            blk = slice(i * SUB, (i + 1) * SUB)
            qhat.append((q[blk] * jnp.exp(cum[blk] - ref)).astype(BF16))
            khat.append((k * jnp.exp(jnp.minimum(ref - cum, EXP_CLAMP))).astype(BF16))
    outs = []
    for h in range(HG_HEADS):
        sl = slice(h * HG_DK, (h + 1) * HG_DK)
        st = st_ref[h]
        if need_o:
            o = lax.dot_general(qin[:, sl], st.astype(BF16), NT_DIMS, preferred_element_type=F32)
            att = jnp.concatenate(
                [lax.dot_general(qhat[i][:, sl], khat[i][:, sl], NT_DIMS, preferred_element_type=F32)
                 for i in range(c // SUB)], axis=0)
            att = jnp.where(mask, att, 0.0)
            o = o + jnp.dot(att.astype(BF16), vb[:, sl], preferred_element_type=F32)
            outs.append(o)
        upd = lax.dot_general(vb[:, sl], kst[:, sl], TN_DIMS, preferred_element_type=F32)
        st_ref[h] = dec[:, sl] * st + upd
    return jnp.concatenate(outs, axis=1) if need_o else None


def _hgrn_kernel(*refs, reverse, mode, nchunk):
    if mode == "state":
        f_ref, v_ref, s0_ref, lb_ref, tri_ref, sfin_ref, st_ref = refs
        q_ref = None
    elif mode == "fwd":
        f_ref, v_ref, q_ref, s0_ref, lb_ref, tri_ref, o_ref, st_ref = refs
    else:
        (f_ref, v_ref, q_ref, og_ref, op_ref, s0_ref, lb_ref, tri_ref, nw_ref, wa_ref,
         y_ref, st_ref, gs_ref) = refs
    j = pl.program_id(1)

    @pl.when(j == 0)
    def _():
        st_ref[...] = s0_ref[0]

    lb = lb_ref[...]
    tri = tri_ref[...]

    def body(ci, carry):
        cc = nchunk - 1 - ci if reverse else ci
        r0 = pl.multiple_of(cc * CHUNK, CHUNK)
        rs = pl.ds(r0, CHUNK)
        fraw = f_ref[0, rs, :].astype(F32)
        v = v_ref[0, rs, :].astype(F32)
        if mode == "state":
            _hgrn_chunk(fraw, v, None, lb, tri, st_ref, reverse, False)
            return carry
        q = _silu(q_ref[0, rs, :].astype(F32))
        o = _hgrn_chunk(fraw, v, q, lb, tri, st_ref, reverse, True)
        if mode == "fwd":
            o_ref[0, rs, :] = o.astype(BF16)
            return carry
        o = o + op_ref[0, rs, :].astype(F32)
        nw = nw_ref[...]
        parts = []
        for h in range(HG_HEADS):
            oh = o[:, h * HG_DK:(h + 1) * HG_DK]
            ms = jnp.mean(oh * oh, -1, keepdims=True)
            parts.append(oh * lax.rsqrt(ms + LN_EPS) * nw)
        on = jnp.concatenate(parts, axis=1)
        gs_ref[rs, :] = (on * _silu(og_ref[0, rs, :].astype(F32))).astype(BF16)
        return carry

    lax.fori_loop(0, nchunk, body, 0, unroll=2)

    if mode == "state":
        @pl.when(j == pl.num_programs(1) - 1)
        def _():
            sfin_ref[0] = st_ref[...]
    elif mode == "rev_out":
        y_ref[0] = jnp.dot(gs_ref[...], wa_ref[...], preferred_element_type=F32)


def _tri(n, reverse, block=None):
    block = n if block is None else block
    i = np.arange(n)
    same = (i[:, None] // block) == (i[None, :] // block)
    m = (i[None, :] >= i[:, None]) if reverse else (i[None, :] <= i[:, None])
    return jnp.asarray((m & same).astype(np.float32))


def _hgrn(proj, s0, lb, reverse, mode, tt, o_prev=None, norm_w=None, w_a=None):
    b, t, _ = proj.shape
    nt = t // tt
    hw = HG_WIDTH

    def tmap(col):
        if reverse:
            return lambda bi, j: (bi, nt - 1 - j, col)
        return lambda bi, j: (bi, j, col)

    pspec = lambda col: pl.BlockSpec((1, tt, hw), tmap(col))
    sspec = pl.BlockSpec((1, HG_HEADS, HG_DK, HG_DK), lambda bi, j: (bi, 0, 0, 0))
    cspec = lambda shape: pl.BlockSpec(shape, lambda bi, j: (0,) * len(shape))
    fcol = PB_FB if reverse else PB_FF
    tri = _tri(CHUNK, reverse)
    kern = functools.partial(_hgrn_kernel, reverse=reverse, mode=mode, nchunk=tt // CHUNK)
    st_scr = pltpu.VMEM((HG_HEADS, HG_DK, HG_DK), F32)
    if mode == "state":
        return pl.pallas_call(
            kern, grid=(b, nt),
            in_specs=[pspec(fcol), pspec(PB_IV), sspec, cspec((1, hw)), cspec((CHUNK, CHUNK))],
            out_specs=sspec,
            out_shape=jax.ShapeDtypeStruct(s0.shape, F32),
            scratch_shapes=[st_scr],
            compiler_params=_cparams(("arbitrary", "arbitrary")),
            name="hgrn_state",
        )(proj, proj, s0, lb, tri)
    if mode == "fwd":
        return pl.pallas_call(
            kern, grid=(b, nt),
            in_specs=[pspec(fcol), pspec(PB_IV), pspec(PB_Q), sspec, cspec((1, hw)), cspec((CHUNK, CHUNK))],
            out_specs=pl.BlockSpec((1, tt, hw), tmap(0)),
            out_shape=jax.ShapeDtypeStruct((b, t, hw), BF16),
            scratch_shapes=[st_scr],
            compiler_params=_cparams(("arbitrary", "arbitrary")),
            name="hgrn_fwd",
        )(proj, proj, proj, s0, lb, tri)
    return pl.pallas_call(
        kern, grid=(b, nt),
        in_specs=[pspec(fcol), pspec(PB_IV), pspec(PB_Q), pspec(PB_OG),
                  pl.BlockSpec((1, tt, hw), tmap(0)), sspec, cspec((1, hw)), cspec((CHUNK, CHUNK)),
                  cspec((1, HG_DK)), cspec((hw, D_MODEL))],
        out_specs=pl.BlockSpec((1, tt, D_MODEL), tmap(0)),
        out_shape=jax.ShapeDtypeStruct((b, t, D_MODEL), F32),
        scratch_shapes=[st_scr, pltpu.VMEM((tt, hw), BF16)],
        compiler_params=_cparams(("arbitrary", "arbitrary")),
        name="hgrn_rev",
    )(proj, proj, proj, proj, o_prev, s0, lb, tri, norm_w, w_a)


def _conv_silu(xp, w, b):
    r = xp.shape[0]
    rio = lax.broadcasted_iota(I32, (r, 1), 0)
    acc = xp * w[MA_CONV // 2:MA_CONV // 2 + 1] + b
    for kk in range(MA_CONV):
        d = kk - MA_CONV // 2
        if d == 0:
            continue
        sh = pltpu.roll(xp, (-d) % r, axis=0)
        valid = jnp.logical_and(rio + d >= 0, rio + d < r)
        acc = acc + jnp.where(valid, sh, 0.0) * w[kk:kk + 1]
    return _silu(acc)


def _expand(x, e):
    hi = x.astype(BF16)
    lo = (x - hi.astype(F32)).astype(BF16)
    return (jnp.dot(hi, e, preferred_element_type=F32) + jnp.dot(lo, e, preferred_element_type=F32))


def _ssd_kernel(*refs, reverse, mode, nrows):
    if mode == "state":
        (x0_ref, x1_ref, b_ref, c_ref, dt_ref, h0_ref, cw_ref, cb_ref, dtb_ref, a_ref, e_ref,
         tri_ref, hfin_ref, hst_ref) = refs
    elif mode == "fwd":
        (x0_ref, x1_ref, b_ref, c_ref, dt_ref, h0_ref, cw_ref, cb_ref, dtb_ref, a_ref, e_ref,
         tri_ref, y_ref, hst_ref) = refs
    else:
        (x0_ref, x1_ref, b_ref, c_ref, dt_ref, h0_ref, cw_ref, cb_ref, dtb_ref, a_ref, e_ref,
         tri_ref, z0_ref, z1_ref, yf_ref, dsk_ref, nw_ref, wb_ref, y_ref, hst_ref, ys_ref) = refs
    j = pl.program_id(1)

    @pl.when(j == 0)
    def _():
        hst_ref[...] = h0_ref[0]

    cw = cw_ref[...]
    cb = cb_ref[...]
    need_y = mode != "state"
    pieces = []
    for pi, pref in enumerate((x0_ref, x1_ref, b_ref, c_ref)):
        if pi == 3 and not need_y:
            pieces.append(None)
            continue
        cs = slice(pi * 1024, (pi + 1) * 1024)
        pieces.append(_conv_silu(pref[0, 0].astype(F32), cw[:, cs], cb[:, cs]))
    xh = jnp.concatenate(pieces[:2], axis=1)
    bm = pieces[2].astype(BF16)
    cm = pieces[3].astype(BF16) if need_y else None

    x = dt_ref[0, 0] + dtb_ref[...]
    dt = jnp.maximum(x, 0.0) + jnp.log1p(jnp.exp(-jnp.abs(x)))
    la = dt * a_ref[...]
    e_mat = e_ref[...]
    tri = tri_ref[...]
    lane0 = MA_HEADS if reverse else 0
    rows = lax.broadcasted_iota(I32, (CHUNK, CHUNK), 0)
    cols = lax.broadcasted_iota(I32, (CHUNK, CHUNK), 1)
    mask = (cols >= rows) if reverse else (cols <= rows)
    lane128 = lax.broadcasted_iota(I32, (CHUNK, 128), 1)

    npair = nrows // 128
    for pbi in range(npair):
        pb = npair - 1 - pbi if reverse else pbi
        prow = slice(pb * 128, (pb + 1) * 128)
        lcum2 = jnp.dot(tri, la[prow], precision=HI, preferred_element_type=F32)
        lcum_t = lcum2.T
        dt_t = dt[prow].T
        for ci in range(2):
            cc = 1 - ci if reverse else ci
            crow = slice(pb * 128 + cc * CHUNK, pb * 128 + (cc + 1) * CHUNK)
            ccol = slice(cc * CHUNK, (cc + 1) * CHUNK)
            lc = lcum2[ccol]
            e = 0 if reverse else CHUNK - 1
            llast = lc[e:e + 1]
            xh_c = xh[crow]
            xb_c = xh_c.astype(BF16)
            w_out = _expand(dt[crow] * jnp.exp(llast - lc), e_mat)
            xd = (xh_c * w_out).astype(BF16)
            expl = _expand(jnp.broadcast_to(jnp.exp(llast), (8, 128)), e_mat)[0:1]
            bc = bm[crow]
            if need_y:
                dec_in = _expand(jnp.exp(lc), e_mat)
                cc_ = cm[crow]
            ys = []
            for g in range(MA_GROUPS):
                gs = slice(g * MA_STATE, (g + 1) * MA_STATE)
                gw = slice(g * MA_GW, (g + 1) * MA_GW)
                hg = hst_ref[g]
                if need_y:
                    cbg = lax.dot_general(cc_[:, gs], bc[:, gs], NT_DIMS, preferred_element_type=F32)
                    pair_out = []
                    for jp in range(2):
                        ms = []
                        for jj in range(2):
                            ln = lane0 + 4 * g + 2 * jp + jj
                            diff = lc[:, ln:ln + 1] - lcum_t[ln:ln + 1, ccol]
                            lmat = jnp.exp(jnp.where(mask, diff, NEG))
                            ms.append(cbg * lmat * dt_t[ln:ln + 1, ccol])
                        lhs = jnp.concatenate(ms, axis=0).astype(BF16)
                        xs = xb_c[:, g * MA_GW + jp * 128:g * MA_GW + (jp + 1) * 128]
                        res = jnp.dot(lhs, xs, preferred_element_type=F32)
                        pair_out.append(jnp.where(lane128 < MA_HEAD_DIM, res[:CHUNK], res[CHUNK:]))
                    y_intra = jnp.concatenate(pair_out, axis=1)
                    y_inter = jnp.dot(cc_[:, gs], hg.astype(BF16), preferred_element_type=F32)
                    ys.append(y_intra + y_inter * dec_in[:, gw])
                upd = lax.dot_general(bc[:, gs], xd[:, gw], TN_DIMS, preferred_element_type=F32)
                hst_ref[g] = expl[:, gw] * hg + upd
            if not need_y:
                continue
            y = jnp.concatenate(ys, axis=1)
            if mode == "fwd":
                y_ref[0, 0, crow, :] = y.astype(BF16)
            else:
                ytot = yf_ref[0, 0, crow, :].astype(F32) + y + xh_c * dsk_ref[...]
                zc = jnp.concatenate([z0_ref[0, 0, crow, :], z1_ref[0, 0, crow, :]], axis=1).astype(F32)
                yz = ytot * _silu(zc)
                nw = nw_ref[...]
                parts = []
                for g in range(MA_GROUPS):
                    gw = slice(g * MA_GW, (g + 1) * MA_GW)
                    yg = yz[:, gw]
                    ms_ = jnp.mean(yg * yg, -1, keepdims=True)
                    parts.append(yg * lax.rsqrt(ms_ + LN_EPS) * nw[:, gw])
                ys_ref[crow, :] = jnp.concatenate(parts, axis=1).astype(BF16)

    if mode == "state":
        @pl.when(j == pl.num_programs(1) - 1)
        def _():
            hfin_ref[0] = hst_ref[...]
    elif mode == "rev_out":
        y_ref[0, 0] = jnp.dot(ys_ref[...], wb_ref[...], preferred_element_type=F32)


def _ssd(pcm, dtraw, h0, consts, reverse, mode, y_f=None, extra=None):
    b, ncols, nrows, _ = pcm.shape
    conv_w, conv_b, dt_bias, a128, e_mat = consts

    def wmap(col):
        if reverse:
            return lambda bi, j: (bi, ncols - 1 - j, 0, col)
        return lambda bi, j: (bi, j, 0, col)

    pspec = lambda col: pl.BlockSpec((1, 1, nrows, 1024), wmap(col))
    dspec = pl.BlockSpec((1, 1, nrows, DT_W), wmap(0))
    hspec = pl.BlockSpec((1, MA_GROUPS, MA_STATE, MA_GW), lambda bi, j: (bi, 0, 0, 0))
    cspec = lambda shape: pl.BlockSpec(shape, lambda bi, j: (0,) * len(shape))
    tri = _tri(128, reverse, block=CHUNK)
    kern = functools.partial(_ssd_kernel, reverse=reverse, mode=mode, nrows=nrows)
    base_specs = [pspec(PC_X0), pspec(PC_X1), pspec(PC_B), pspec(PC_C), dspec, hspec,
                  cspec((MA_CONV, 4096)), cspec((1, 4096)), cspec((1, DT_W)), cspec((1, DT_W)),
                  cspec((DT_W, MA_INNER)), cspec((128, 128))]
    base_args = [pcm, pcm, pcm, pcm, dtraw, h0, conv_w, conv_b, dt_bias, a128, e_mat, tri]
    h_scr = pltpu.VMEM((MA_GROUPS, MA_STATE, MA_GW), F32)
    sem = _cparams(("arbitrary", "arbitrary"))
    if mode == "state":
        return pl.pallas_call(
            kern, grid=(b, ncols), in_specs=base_specs, out_specs=hspec,
            out_shape=jax.ShapeDtypeStruct(h0.shape, F32), scratch_shapes=[h_scr],
            compiler_params=sem, name="ssd_state")(*base_args)
    if mode == "fwd":
        return pl.pallas_call(
            kern, grid=(b, ncols), in_specs=base_specs,
            out_specs=pl.BlockSpec((1, 1, nrows, MA_INNER), wmap(0)),
            out_shape=jax.ShapeDtypeStruct((b, ncols, nrows, MA_INNER), BF16),
            scratch_shapes=[h_scr], compiler_params=sem, name="ssd_fwd")(*base_args)
    d_skip, norm_w, w_b = extra
    return pl.pallas_call(
        kern, grid=(b, ncols),
        in_specs=base_specs + [pspec(PC_Z0), pspec(PC_Z1),
                               pl.BlockSpec((1, 1, nrows, MA_INNER), wmap(0)),
                               cspec((1, MA_INNER)), cspec((1, MA_INNER)), cspec((MA_INNER, D_MODEL))],
        out_specs=pl.BlockSpec((1, 1, nrows, D_MODEL), wmap(0)),
        out_shape=jax.ShapeDtypeStruct((b, ncols, nrows, D_MODEL), F32),
        scratch_shapes=[h_scr, pltpu.VMEM((nrows, MA_INNER), BF16)],
        compiler_params=sem, name="ssd_rev")(*base_args, pcm, pcm, y_f, d_skip, norm_w, w_b)


def _merge_kernel(x_ref, ya_ref, yb_ref, perm_ref, ga_ref, gb_ref, g1_ref, wo_ref, lg_ref, lb_ref, o_ref):
    tm, d = x_ref.shape
    yb = yb_ref[...].reshape(tm, d)
    hi = yb.astype(BF16)
    lo = (yb - hi.astype(F32)).astype(BF16)
    perm = perm_ref[...]
    yb = (jnp.dot(perm, hi, preferred_element_type=F32) + jnp.dot(perm, lo, preferred_element_type=F32))
    m = (_sigmoid(ga_ref[...].astype(F32)) * ya_ref[...] + _sigmoid(gb_ref[...].astype(F32)) * yb)
    out = jnp.dot(m.astype(BF16), wo_ref[...], preferred_element_type=F32)
    z = DN_ALPHA * x_ref[...] + g1_ref[0] * out
    o_ref[...] = _layer_norm(z) * lg_ref[...] + lb_ref[...]


def _merge(x2, y_a, y_b_cm, proj_rm, g1, w_out, ln_g, ln_b, tokens_per_batch):
    m, d = x2.shape
    ncols = y_b_cm.shape[1]
    tm = COL_TILE * ncols
    tpb = tokens_per_batch // tm
    tspec = pl.BlockSpec((tm, d), lambda i: (i, 0))
    cspec = lambda shape: pl.BlockSpec(shape, lambda i: (0,) * len(shape))
    perm = jnp.asarray(_perm_matrix(ncols, COL_TILE)).astype(BF16)
    return pl.pallas_call(
        _merge_kernel, grid=(m // tm,),
        in_specs=[tspec, tspec,
                  pl.BlockSpec((1, ncols, COL_TILE, d), lambda i: (i // tpb, 0, i % tpb, 0)),
                  cspec((tm, tm)),
                  pl.BlockSpec((tm, d), lambda i: (i, PB_GA)),
                  pl.BlockSpec((tm, d), lambda i: (i, PB_GB)),
                  pl.BlockSpec((1, 1, d), lambda i: (i // tpb, 0, 0)),
                  cspec((d, d)), cspec((1, d)), cspec((1, d))],
        out_specs=tspec,
        out_shape=jax.ShapeDtypeStruct((m, d), F32),
        compiler_params=_cparams(("arbitrary",)),
        name="merge",
    )(x2, y_a, y_b_cm, perm, proj_rm, proj_rm, g1, w_out, ln_g, ln_b)


KEY_BIAS = 0x40000000
KEY_MIN = 1.1754944e-38
LOW_MASK = PEER_NKEYS - 1


def _sort_pairs(n):
    def merge(lo, hi, r):
        step = r * 2
        if step < hi - lo:
            yield from merge(lo, hi, step)
            yield from merge(lo + r, hi, step)
            for i in range(lo + r, hi - r, step):
                yield (i, i + r)
        else:
            yield (lo, lo + r)

    def sort(lo, hi):
        if hi - lo >= 1:
            mid = lo + (hi - lo) // 2
            yield from sort(lo, mid)
            yield from sort(mid + 1, hi)
            yield from merge(lo, hi, 1)

    return list(sort(0, n - 1))


def _merge_pairs(n):
    out, j = [], n // 2
    while j:
        out += [(i, i | j) for i in range(n) if not i & j]
        j //= 2
    return out


def _cex(v, pairs):
    v = list(v)
    for i, j in pairs:
        v[i], v[j] = jnp.maximum(v[i], v[j]), jnp.minimum(v[i], v[j])
    return v


def _cex_p(v, c, pairs):
    v, c = list(v), list(c)
    for i, j in pairs:
        m = v[i] >= v[j]
        v[i], v[j], c[i], c[j] = (jnp.maximum(v[i], v[j]), jnp.minimum(v[i], v[j]),
                                  jnp.where(m, c[i], c[j]), jnp.where(m, c[j], c[i]))
    return v, c


def _to_key(s, low):
    b = pltpu.bitcast(s, I32)
    k = jnp.where(b < 0, jnp.bitwise_xor(b, 0x7FFFFFFF), b)
    k = jnp.right_shift(k, 1) + KEY_BIAS
    k = jnp.bitwise_or(jnp.bitwise_and(k, ~LOW_MASK), low)
    return pltpu.bitcast(k, F32)


def _from_key(kf):
    k = pltpu.bitcast(kf, I32)
    low = jnp.bitwise_and(k, LOW_MASK)
    k = jnp.left_shift(jnp.bitwise_and(k, ~LOW_MASK) - KEY_BIAS, 1)
    b = jnp.where(k < 0, jnp.bitwise_xor(k, 0x7FFFFFFF), k)
    return pltpu.bitcast(b, F32), low


def _top16_keys(st, sub):
    v = []
    for r in range(PEER_NKEYS // 8):
        v.append(_to_key(st[8 * r:8 * r + 8], (LOW_MASK - 8 * r) - sub))
    v = _cex(v, _sort_pairs(16))
    for sh in (4, 2, 1):
        part = [pltpu.roll(x, sh, axis=0) for x in v]
        v = _cex([jnp.maximum(v[r], part[15 - r]) for r in range(16)], _merge_pairs(16))
    return v


def _natural(lst, half, sub):
    x = lst[8 * half]
    for s_ in range(1, 8):
        x = jnp.where(sub == s_, lst[8 * half + s_], x)
    return x


def _slot_table():
    slots = {}
    for s_ in range(8):
        slots[(0, s_)] = (0, s_)
        slots[(1, s_)] = (0, 8 + s_)
        slots[(3, s_)] = (8 + s_, 0)
    for s_ in range(1, 8):
        slots[(2, s_)] = (s_, 0)
        slots[(4, s_)] = (1, s_)
    for s_ in range(2, 8):
        slots[(5, s_)] = (s_, 1)
    for s_ in (2, 3, 4):
        slots[(6, s_)] = (2, s_)
    for s_ in (2, 3):
        slots[(7, s_)] = (3, s_)
    slots[(7, 4)] = (4, 2)
    want = sorted((j, l) for j in range(PEER_TOPK) for l in range(PEER_TOPK) if (j + 1) * (l + 1) <= PEER_TOPK)
    assert sorted(slots.values()) == want
    low = -np.ones((8, 8), np.int32)
    for (q, s_), jl in slots.items():
        low[q, s_] = LOW_MASK - want.index(jl)
    return jnp.asarray(np.repeat(low.reshape(64, 1), 128, axis=1))


def _retrieve_head(q_ref, sk_ref, slot_low, sub, t0, h):
    vals, idxs = [], []
    for i in range(2):
        cidx = 2 * h + i
        qc = q_ref[pl.ds(t0, 128), cidx * PEER_HALF:(cidx + 1) * PEER_HALF]
        st = lax.dot_general(sk_ref[cidx], qc, NT_DIMS, preferred_element_type=F32)
        dec = [_from_key(k) for k in _top16_keys(st, sub)]
        vals.append([d[0] for d in dec])
        idxs.append([LOW_MASK - d[1] for d in dec])
    v1, v2 = vals
    i1, i2 = idxs
    s1n = [_natural(v1, hf, sub) for hf in range(2)]
    s2n = [_natural(v2, hf, sub) for hf in range(2)]
    i1n = [_natural(i1, hf, sub) * PEER_NKEYS for hf in range(2)]
    i2n = [_natural(i2, hf, sub) for hf in range(2)]
    row4 = sub == 4
    cand = [(v1[0] + s2n[0], i1[0] * PEER_NKEYS + i2n[0]),
            (v1[0] + s2n[1], i1[0] * PEER_NKEYS + i2n[1]),
            (s1n[0] + v2[0], i1n[0] + i2[0]),
            (s1n[1] + v2[0], i1n[1] + i2[0]),
            (v1[1] + s2n[0], i1[1] * PEER_NKEYS + i2n[0]),
            (s1n[0] + v2[1], i1n[0] + i2[1]),
            (v1[2] + s2n[0], i1[2] * PEER_NKEYS + i2n[0]),
            (jnp.where(row4, s1n[0] + v2[2], v1[3] + s2n[0]),
             jnp.where(row4, i1n[0] + i2[2], i1[3] * PEER_NKEYS + i2n[0]))]
    keys, codes = [], []
    for qi, (cv, cc) in enumerate(cand):
        low = slot_low[8 * qi:8 * qi + 8]
        keys.append(jnp.where(low >= 0, _to_key(cv, low), KEY_MIN))
        codes.append(cc)
    keys, codes = _cex_p(keys, codes, _sort_pairs(8))
    pk = [pltpu.roll(x, 4, axis=0) for x in keys]
    pc = [pltpu.roll(x, 4, axis=0) for x in codes]
    keys, codes = _cex_p(keys + pk[::-1], codes + pc[::-1], _merge_pairs(16))
    for sh in (2, 1):
        pk = [pltpu.roll(x, sh, axis=0) for x in keys]
        pc = [pltpu.roll(x, sh, axis=0) for x in codes]
        mk, mc = [], []
        for r in range(16):
            m = keys[r] >= pk[15 - r]
            mk.append(jnp.maximum(keys[r], pk[15 - r]))
            mc.append(jnp.where(m, codes[r], pc[15 - r]))
        keys, codes = _cex_p(mk, mc, _merge_pairs(16))
    tv = [_from_key(k)[0] for k in keys]
    ex = [jnp.exp(t - tv[0]) for t in tv]
    den = ex[0]
    for e_ in ex[1:]:
        den = den + e_
    inv = 1.0 / den
    return codes, [e_ * inv for e_ in ex]


def _peer_kernel(x_ref, sh_ref, sc_ref, g2_ref, wq_ref, sk_ref, slot_ref, u_ref, v_ref, lg_ref, lb_ref,
                 o_ref, xm_ref, q_ref, acc_ref, t_ref, tc_ref, tg_ref, ai_ref, bi_ref, gt_ref):
    s = pl.program_id(1)
    p = x_ref.shape[0]

    @pl.when(s == 0)
    def _():
        xm = (_layer_norm(x_ref[...]) * (1.0 + sc_ref[0]) + sh_ref[0]).astype(BF16)
        xm_ref[...] = xm
        acc_ref[...] = jnp.zeros_like(acc_ref)
        q_ref[...] = jnp.dot(xm, wq_ref[...], preferred_element_type=F32).astype(BF16)
        slot_low = slot_ref[...]
        sub = lax.broadcasted_iota(I32, (8, 128), 0)

        def tile(ti, carry):
            t0 = pl.multiple_of(ti * 128, 128)
            for h in range(PEER_HEADS):
                codes, gates = _retrieve_head(q_ref, sk_ref, slot_low, sub, t0, h)
                for r in range(PEER_TOPK):
                    row = h * PEER_TOPK + r
                    tc_ref[row:row + 1, :] = codes[r][0:1].astype(F32)
                    tg_ref[row:row + 1, :] = gates[r][0:1]
            ct = tc_ref[...].T.astype(I32)
            ai_ref[pl.ds(t0, 128), :] = jnp.right_shift(ct, 7).astype(F32)
            bi_ref[pl.ds(t0, 128), :] = jnp.bitwise_and(ct, LOW_MASK).astype(F32)
            gt_ref[pl.ds(t0, 128), :] = tg_ref[...].T
            return carry

        lax.fori_loop(0, p // 128, tile, 0)

        sub128 = lax.broadcasted_iota(I32, (PEER_NKEYS, 128), 0)
        amap = jnp.bitwise_or(jnp.left_shift(jnp.bitwise_and(sub128, 63), 1),
                              jnp.right_shift(sub128, 6)).astype(F32)
        bmap = sub128.astype(F32)

        def tok8(i, carry):
            t0 = pl.multiple_of(i * TOK_UNROLL, TOK_UNROLL)
            a8 = ai_ref[pl.ds(t0, TOK_UNROLL), :]
            b8 = bi_ref[pl.ds(t0, TOK_UNROLL), :]
            g8 = gt_ref[pl.ds(t0, TOK_UNROLL), :]
            for j in range(TOK_UNROLL):
                at = jnp.where(a8[j:j + 1] == amap, 1.0, 0.0).astype(BF16)
                rt = jnp.where(b8[j:j + 1] == bmap, g8[j:j + 1], 0.0).astype(BF16)
                w = lax.dot_general(at, rt, NT_DIMS, preferred_element_type=F32)
                lo = pltpu.bitcast(w[:64], I32) + 0x8000
                hi = pltpu.bitcast(w[64:], I32) + 0x8000
                lo = lax.shift_right_logical(lo, 16)
                hi = jnp.bitwise_and(hi, -65536)
                r0 = pl.multiple_of((t0 + j) * PEER_PITCH, 8)
                t_ref[pl.ds(r0, 64), :] = jnp.bitwise_or(lo, hi)
            return carry

        lax.fori_loop(0, p // TOK_UNROLL, tok8, 0)

    xm = xm_ref[...]
    tot = None
    for g in range(PEER_G):
        a = s * PEER_G + g
        wp = t_ref[pl.ds(a, p, stride=PEER_PITCH), :]
        wlo = pltpu.bitcast(jnp.left_shift(wp, 16), F32)
        whi = pltpu.bitcast(jnp.bitwise_and(wp, -65536), F32)
        wgt = jnp.concatenate([wlo, whi], axis=1)
        rows = slice(g * 2 * PEER_NKEYS, (g + 1) * 2 * PEER_NKEYS)
        hval = lax.dot_general(xm, u_ref[rows, :], NT_DIMS, preferred_element_type=F32)
        act = 0.5 * hval * (1.0 + lax.erf(hval * 0.7071067811865476))
        z = (wgt * act).astype(BF16)
        d = jnp.dot(z, v_ref[rows, :], preferred_element_type=F32)
        tot = d if tot is None else tot + d
    acc_ref[...] += tot

    @pl.when(s == pl.num_programs(1) - 1)
    def _():
        z = DN_ALPHA * x_ref[...] + g2_ref[0] * acc_ref[...]
        o_ref[...] = _layer_norm(z) * lg_ref[...] + lb_ref[...]


def _peer(x1, shift, scale, g2, wq, sk, u2, v2, ln_g, ln_b, tokens_per_batch):
    m, d = x1.shape
    p = PEER_P
    tpb = tokens_per_batch // p
    blk = PEER_G * 2 * PEER_NKEYS
    nsteps = u2.shape[0] // blk
    slot = _slot_table()
    cspec = lambda shape: pl.BlockSpec(shape, lambda i, s: (0,) * len(shape))
    bspec = pl.BlockSpec((1, 1, d), lambda i, s: (i // tpb, 0, 0))
    nent = PEER_HEADS * PEER_TOPK
    return pl.pallas_call(
        _peer_kernel, grid=(m // p, nsteps),
        in_specs=[pl.BlockSpec((p, d), lambda i, s: (i, 0)), bspec, bspec, bspec,
                  cspec(wq.shape), cspec(sk.shape), cspec(slot.shape),
                  pl.BlockSpec((blk, d), lambda i, s: (s, 0)),
                  pl.BlockSpec((blk, d), lambda i, s: (s, 0)),
                  cspec((1, d)), cspec((1, d))],
        out_specs=pl.BlockSpec((p, d), lambda i, s: (i, 0)),
        out_shape=jax.ShapeDtypeStruct((m, d), F32),
        scratch_shapes=[pltpu.VMEM((p, d), BF16), pltpu.VMEM((p, 2 * PEER_HEADS * PEER_HALF), BF16),
                        pltpu.VMEM((p, d), F32), pltpu.VMEM((p * PEER_PITCH, 128), I32),
                        pltpu.VMEM((nent, 128), F32), pltpu.VMEM((nent, 128), F32),
                        pltpu.VMEM((p, nent), F32), pltpu.VMEM((p, nent), F32), pltpu.VMEM((p, nent), F32)],
        compiler_params=_cparams(("arbitrary", "arbitrary")),
        name="peer",
    )(x1, shift, scale, g2, wq, sk, slot, u2, v2, ln_g, ln_b)


def _layer(x, c, ctx, c_ctx, lw, lb_l):
    b, t, d = x.shape
    tc = ctx.shape[1]

    npad = -(-(b + 1) // 8) * 8
    c_all = jnp.zeros((npad, d), F32).at[:b].set(c).at[b].set(c_ctx)
    mod = _ada(c_all, lw["w_ada"], lw["b_ada"])
    sh1, sc1, g1, sh2, sc2, g2 = [mod[:b, i * d:(i + 1) * d].reshape(b, 1, d) for i in range(6)]
    csh1 = jnp.broadcast_to(mod[b, 0:d], (b, 1, d))
    csc1 = jnp.broadcast_to(mod[b, d:2 * d], (b, 1, d))

    w_in = lw["w_in"]
    off = np.cumsum([0, HG_WIDTH, HG_WIDTH, HG_WIDTH, MA_INNER + MA_GROUPS * MA_STATE, MA_HEADS, MA_HEADS,
                     HG_WIDTH, HG_WIDTH, MA_GROUPS * MA_STATE, MA_INNER, D_MODEL, D_MODEL])
    o_ff, o_fb, o_iv, o_xb, o_dtf, o_dtb, o_q, o_og, o_cm, o_z, o_ga, o_gb = off[:12]
    col = lambda o, n: w_in[:, o:o + n]
    w_rm = jnp.concatenate([col(o_ff, 1024), col(o_fb, 1024), col(o_iv, 1024), col(o_q, 1024),
                            col(o_og, 1024), col(o_ga, 1024), col(o_gb, 1024)], axis=1).astype(BF16)
    w_cm = jnp.concatenate([col(o_xb, 3072), col(o_cm, 1024), col(o_z, 2048)], axis=1).astype(BF16)
    w_dt = jnp.concatenate([col(o_dtf, 2 * MA_HEADS), jnp.zeros((d, DT_W - 2 * MA_HEADS), F32)],
                           axis=1).astype(BF16)
    lb_f = lb_l[0].reshape(1, HG_WIDTH)
    lb_b = lb_l[1].reshape(1, HG_WIDTH)
    zpad = jnp.zeros((DT_W - 2 * MA_HEADS,), F32)
    dt_bias = jnp.concatenate([lw["ma_dt_bias"].reshape(-1), zpad]).reshape(1, DT_W)
    a128 = jnp.concatenate([-jnp.exp(lw["ma_a_log"].astype(F32)).reshape(-1), zpad]).reshape(1, DT_W)
    conv_w = lw["ma_conv_w"].T
    conv_b = lw["ma_conv_b"].reshape(1, -1)
    head_of = np.arange(MA_INNER) // MA_HEAD_DIM
    e_f = jnp.asarray((np.arange(DT_W)[:, None] == head_of[None, :]).astype(np.float32)).astype(BF16)
    e_b = jnp.asarray((np.arange(DT_W)[:, None] == (head_of + MA_HEADS)[None, :]).astype(np.float32)).astype(BF16)
    consts_f = (conv_w, conv_b, dt_bias, a128, e_f)
    consts_b = (conv_w, conv_b, dt_bias, a128, e_b)
    d_skip = jnp.repeat(lw["ma_d"], MA_HEAD_DIM).reshape(1, MA_INNER)
    ma_norm_w = lw["ma_norm_w"].reshape(1, MA_INNER)
    hg_norm_w = lw["hg_norm_w"].reshape(1, HG_DK)
    w_a = lw["w_branch_a"].astype(BF16)
    w_b = lw["w_branch_b"].astype(BF16)
    w_o = lw["w_out"].astype(BF16)

    projc = _inproj_rm(ctx.reshape(b * tc, d), csh1, csc1, w_rm, tc, tc).reshape(b, tc, NP_RM)
    pcmc, dtc = _inproj_cm(ctx, csh1, csc1, w_cm, w_dt, 1)
    zs = jnp.zeros((b, HG_HEADS, HG_DK, HG_DK), F32)
    zh = jnp.zeros((b, MA_GROUPS, MA_STATE, MA_GW), F32)
    s_hf = _hgrn(projc, zs, lb_f, False, "state", tc)
    s_hb = _hgrn(projc, zs, lb_b, True, "state", tc)
    h_f = _ssd(pcmc, dtc, zh, consts_f, False, "state")
    h_b = _ssd(pcmc, dtc, zh, consts_b, True, "state")

    x2 = x.reshape(b * t, d)
    proj = _inproj_rm(x2, sh1, sc1, w_rm, t, min(1024, t))
    pcm, dtr = _inproj_cm(x, sh1, sc1, w_cm, w_dt, GRID_W)
    proj3 = proj.reshape(b, t, NP_RM)
    tt = min(512, t)
    o_f = _hgrn(proj3, s_hf, lb_f, False, "fwd", tt)
    y_a = _hgrn(proj3, s_hb, lb_b, True, "rev_out", tt, o_prev=o_f, norm_w=hg_norm_w, w_a=w_a)
    y_f = _ssd(pcm, dtr, h_f, consts_f, False, "fwd")
    y_b = _ssd(pcm, dtr, h_b, consts_b, True, "rev_out", y_f=y_f, extra=(d_skip, ma_norm_w, w_b))
    x1 = _merge(x2, y_a.reshape(b * t, d), y_b, proj, g1, w_o,
                lw["ln1_g"].reshape(1, d), lw["ln1_b"].reshape(1, d), t)

    wq = lw["peer_wq"].astype(BF16)
    sk = lw["peer_subkeys"].reshape(PEER_HEADS * 2, PEER_NKEYS, PEER_HALF).astype(BF16)
    u2 = lw["peer_u"].astype(BF16)
    v2 = lw["peer_v"].astype(BF16)
    out = _peer(x1, sh2, sc2, g2, wq, sk, u2, v2, lw["ln2_g"].reshape(1, d), lw["ln2_b"].reshape(1, d), t)
    return out.reshape(b, t, d)


def kernel(x, c, ctx, c_ctx, w_ada, b_ada, w_in, hg_lb_logits, hg_norm_w, ma_conv_w, ma_conv_b, ma_dt_bias,
           ma_a_log, ma_d, ma_norm_w, w_branch_a, w_branch_b, w_out, ln1_g, ln1_b, peer_wq, peer_subkeys,
           peer_u, peer_v, ln2_g, ln2_b):
    depth = w_in.shape[0]
    assert depth == 1, "the scan states of a single (last) layer are implemented"
    lb_all = jnp.cumsum(jax.nn.softmax(hg_lb_logits.astype(F32), axis=1), axis=1)
    lw = {"w_ada": w_ada[0], "b_ada": b_ada[0], "w_in": w_in[0], "hg_norm_w": hg_norm_w[0],
          "ma_conv_w": ma_conv_w[0], "ma_conv_b": ma_conv_b[0], "ma_dt_bias": ma_dt_bias[0],
          "ma_a_log": ma_a_log[0], "ma_d": ma_d[0], "ma_norm_w": ma_norm_w[0],
          "w_branch_a": w_branch_a[0], "w_branch_b": w_branch_b[0], "w_out": w_out[0],
          "ln1_g": ln1_g[0], "ln1_b": ln1_b[0], "peer_wq": peer_wq[0], "peer_subkeys": peer_subkeys[0],
          "peer_u": peer_u[0], "peer_v": peer_v[0], "ln2_g": ln2_g[0], "ln2_b": ln2_b[0]}
    return _layer(x, c, ctx, c_ctx, lw, lb_all[:, 0])
```

```python
import functools

import numpy as np
import jax
import jax.numpy as jnp
from jax import lax
from jax.experimental import pallas as pl
from jax.experimental.pallas import tpu as pltpu

F32 = jnp.float32
BF16 = jnp.bfloat16
I32 = jnp.int32
HI = lax.Precision.HIGHEST

D_MODEL = 1024
GRID_W = 64
CHUNK = 64
SUB = 16
HG_NB = 2
HG_HEADS = 8
HG_DK = 128
HG_WIDTH = HG_HEADS * HG_DK
MA_INNER = 2 * D_MODEL
MA_HEAD_DIM = 64
MA_HEADS = MA_INNER // MA_HEAD_DIM
MA_GROUPS = 8
MA_STATE = 128
MA_GW = MA_INNER // MA_GROUPS
MA_CONV = 5
PEER_HEADS = 8
PEER_NKEYS = 128
PEER_TOPK = 16
PEER_HALF = 128
DN_ALPHA = 2.0 ** 0.25
LN_EPS = 1e-6
NEG = -3.0e38
EXP_CLAMP = 60.0

PB_FF, PB_FB, PB_IV, PB_Q, PB_OG, PB_GA, PB_GB = range(7)
PC_X0, PC_X1, PC_B, PC_C, PC_Z0, PC_Z1 = range(6)
NP_RM = 7 * 1024
NP_CM = 6 * 1024
DT_W = 128
COL_TILE = 8

PEER_P = 512
PEER_G = 4
PEER_PITCH = 72
TOK_UNROLL = 16
VMEM_LIMIT = 56 * 1024 * 1024

NT_DIMS = (((1,), (1,)), ((), ()))
TN_DIMS = (((0,), (0,)), ((), ()))


def _cparams(sem):
    return pltpu.CompilerParams(dimension_semantics=sem, vmem_limit_bytes=VMEM_LIMIT)


def _sigmoid(x):
    return jax.nn.sigmoid(x)


def _silu(x):
    return x * jax.nn.sigmoid(x)


def _dot_hilo(m, x):
    hi = x.astype(BF16)
    lo = (x - hi.astype(F32)).astype(BF16)
    return jnp.dot(m, hi, preferred_element_type=F32) + jnp.dot(m, lo, preferred_element_type=F32)


def _layer_norm(x):
    mu = jnp.mean(x, -1, keepdims=True)
    xc = x - mu
    var = jnp.mean(xc * xc, -1, keepdims=True)
    return xc * lax.rsqrt(var + LN_EPS)


def _ada_kernel(c_ref, w_ref, b_ref, o_ref):
    c = c_ref[...]
    o_ref[...] = jnp.dot(_silu(c), w_ref[...], precision=HI, preferred_element_type=F32) + b_ref[...]


def _ada(c_all, w_ada, b_ada):
    n, d = c_all.shape
    e = w_ada.shape[1]
    tn = 1024
    return pl.pallas_call(
        _ada_kernel,
        grid=(e // tn,),
        in_specs=[pl.BlockSpec((n, d), lambda j: (0, 0)),
                  pl.BlockSpec((d, tn), lambda j: (0, j)),
                  pl.BlockSpec((1, tn), lambda j: (0, j))],
        out_specs=pl.BlockSpec((n, tn), lambda j: (0, j)),
        out_shape=jax.ShapeDtypeStruct((n, e), F32),
        compiler_params=_cparams(("arbitrary",)),
        name="ada",
    )(c_all, w_ada, b_ada.reshape(1, e))


def _inproj_rm_kernel(x_ref, sh_ref, sc_ref, w_ref, o_ref, u_ref):
    @pl.when(pl.program_id(1) == 0)
    def _():
        u = _layer_norm(x_ref[...]) * (1.0 + sc_ref[0]) + sh_ref[0]
        u_ref[...] = u.astype(BF16)

    o_ref[...] = jnp.dot(u_ref[...], w_ref[...], preferred_element_type=F32).astype(BF16)


def _inproj_rm(x2, shift, scale, w_rm, tokens_per_batch, tm):
    m, d = x2.shape
    tpb = tokens_per_batch // tm
    tn = 1024
    return pl.pallas_call(
        _inproj_rm_kernel,
        grid=(m // tm, NP_RM // tn),
        in_specs=[pl.BlockSpec((tm, d), lambda i, j: (i, 0)),
                  pl.BlockSpec((1, 1, d), lambda i, j: (i // tpb, 0, 0)),
                  pl.BlockSpec((1, 1, d), lambda i, j: (i // tpb, 0, 0)),
                  pl.BlockSpec((d, tn), lambda i, j: (0, j))],
        out_specs=pl.BlockSpec((tm, tn), lambda i, j: (i, j)),
        out_shape=jax.ShapeDtypeStruct((m, NP_RM), BF16),
        scratch_shapes=[pltpu.VMEM((tm, d), BF16)],
        compiler_params=_cparams(("arbitrary", "arbitrary")),
        name="inproj_rm",
    )(x2, shift, scale, w_rm)


def _inproj_cm_kernel(*refs, permute):
    if permute:
        x_ref, sh_ref, sc_ref, perm_ref, w_ref, wdt_ref, o_ref, dt_ref, u_ref = refs
    else:
        x_ref, sh_ref, sc_ref, w_ref, wdt_ref, o_ref, dt_ref, u_ref = refs
    tm, d = u_ref.shape

    @pl.when(pl.program_id(2) == 0)
    def _():
        x = x_ref[...].reshape(tm, d)
        ub = (_layer_norm(x) * (1.0 + sc_ref[0]) + sh_ref[0]).astype(BF16)
        if permute:
            ub = jnp.dot(perm_ref[...], ub, preferred_element_type=F32).astype(BF16)
        u_ref[...] = ub
        dt_ref[...] = jnp.dot(ub, wdt_ref[...], preferred_element_type=F32).reshape(dt_ref.shape)

    o_ref[...] = jnp.dot(u_ref[...], w_ref[...], preferred_element_type=F32).astype(BF16).reshape(o_ref.shape)


def _perm_matrix(nrows, ncols):
    n = nrows * ncols
    p = np.zeros((n, n), np.float32)
    r, w = np.meshgrid(np.arange(nrows), np.arange(ncols), indexing="ij")
    p[(w * nrows + r).ravel(), (r * ncols + w).ravel()] = 1.0
    return p


def _inproj_cm(x, shift, scale, w_cm, w_dt, ncols):
    b, t, d = x.shape
    rows = t // ncols
    tn = 1024
    cspec = lambda shape: pl.BlockSpec(shape, lambda bi, wi, j: (0,) * len(shape))
    mspec = pl.BlockSpec((1, 1, d), lambda bi, wi, j: (bi, 0, 0))
    if ncols == 1:
        ct, tm = 1, rows
        xv = x
        xspec = pl.BlockSpec((1, rows, d), lambda bi, wi, j: (bi, 0, 0))
        extra, especs = [], []
    else:
        ct, tm = COL_TILE, rows * COL_TILE
        xv = x.reshape(b, rows, ncols, d)
        xspec = pl.BlockSpec((1, rows, ct, d), lambda bi, wi, j: (bi, 0, wi, 0))
        extra = [jnp.asarray(_perm_matrix(rows, ct)).astype(BF16)]
        especs = [cspec((tm, tm))]
    return pl.pallas_call(
        functools.partial(_inproj_cm_kernel, permute=ncols > 1),
        grid=(b, ncols // ct, NP_CM // tn),
        in_specs=[xspec, mspec, mspec] + especs +
                 [pl.BlockSpec((d, tn), lambda bi, wi, j: (0, j)), cspec((d, DT_W))],
        out_specs=[pl.BlockSpec((1, ct, rows, tn), lambda bi, wi, j: (bi, wi, 0, j)),
                   pl.BlockSpec((1, ct, rows, DT_W), lambda bi, wi, j: (bi, wi, 0, 0))],
        out_shape=[jax.ShapeDtypeStruct((b, ncols, rows, NP_CM), BF16),
                   jax.ShapeDtypeStruct((b, ncols, rows, DT_W), F32)],
        scratch_shapes=[pltpu.VMEM((tm, d), BF16)],
        compiler_params=_cparams(("arbitrary", "arbitrary", "arbitrary")),
        name="inproj_cm",
    )(xv, shift, scale, *extra, w_cm, w_dt)


def _hgrn_chunks(seqs, lb, tri, st_ref, reverse, need_o):
    c = seqs[0][0].shape[0]
    heads = [(bb, h) for bb in range(len(seqs)) for h in range(HG_HEADS)]
    sl = lambda h: slice(h * HG_DK, (h + 1) * HG_DK)
    rows = lax.broadcasted_iota(I32, (c, c), 0)
    cols = lax.broadcasted_iota(I32, (c, c), 1)
    mask = (cols >= rows) if reverse else (cols <= rows)
    tri_b = tri.astype(BF16)
    pre = []
    for fraw, v, q in seqs:
        fg = lb + (1.0 - lb) * _sigmoid(fraw)
        logf = jnp.log(fg)
        pre.append((1.0 - fg, logf, _dot_hilo(tri_b, logf), v.astype(BF16), q))
    e = 0 if reverse else c - 1
    kst, dec, qin, qcat, kcat, vbs = [], [], [], [], [], []
    for k, logf, cum, vb, q in pre:
        glast = cum[e:e + 1]
        kst.append((k * jnp.exp(glast - cum)).astype(BF16))
        dec.append(jnp.exp(glast))
        vbs.append(vb)
        if not need_o:
            continue
        qin.append((q * jnp.exp(cum)).astype(BF16))
        qhat, khat = [], []
        for i in range(c // SUB):
            ei = i * SUB + SUB - 1 if reverse else i * SUB
            ref = cum[ei:ei + 1] - logf[ei:ei + 1]
            blk = slice(i * SUB, (i + 1) * SUB)
            qh = (q[blk] * jnp.exp(cum[blk] - ref)).astype(BF16)
            pads = [jnp.zeros((SUB, qh.shape[1]), BF16)] * (c // SUB)
            pads[i] = qh
            qhat.append(jnp.concatenate(pads, axis=0))
            khat.append((k * jnp.exp(jnp.minimum(ref - cum, EXP_CLAMP))).astype(BF16))
        qcat.append(qhat)
        kcat.append(khat)
    att, inter = {}, {}
    for bb, h in heads:
        st = st_ref[bb, h]
        if need_o:
            inter[bb, h] = lax.dot_general(qin[bb][:, sl(h)], st.astype(BF16), NT_DIMS,
                                           preferred_element_type=F32)
            att[bb, h] = lax.dot_general(jnp.concatenate([x_[:, sl(h)] for x_ in qcat[bb]], axis=1),
                                         jnp.concatenate([x_[:, sl(h)] for x_ in kcat[bb]], axis=1),
                                         NT_DIMS, preferred_element_type=F32)
        upd = lax.dot_general(vbs[bb][:, sl(h)], kst[bb][:, sl(h)], TN_DIMS, preferred_element_type=F32)
        st_ref[bb, h] = dec[bb][:, sl(h)] * st + upd
    if not need_o:
        return None
    outs = [[] for _ in seqs]
    for bb, h in heads:
        a_ = jnp.where(mask, att[bb, h], 0.0).astype(BF16)
        outs[bb].append(inter[bb, h] + jnp.dot(a_, vbs[bb][:, sl(h)], preferred_element_type=F32))
    return [jnp.concatenate(o_, axis=1) for o_ in outs]


def _hgrn_kernel(*refs, reverse, mode, nchunk):
    if mode == "state":
        f_ref, v_ref, s0_ref, lb_ref, tri_ref, sfin_ref, st_ref = refs
        q_ref = None
    elif mode == "fwd":
        f_ref, v_ref, q_ref, s0_ref, lb_ref, tri_ref, o_ref, st_ref = refs
    else:
        (f_ref, v_ref, q_ref, og_ref, op_ref, s0_ref, lb_ref, tri_ref, nw_ref, wa_ref,
         y_ref, st_ref, gs_ref) = refs
    j = pl.program_id(1)
    nb = f_ref.shape[0]

    @pl.when(j == 0)
    def _():
        st_ref[...] = s0_ref[...]

    lb = lb_ref[...]
    tri = tri_ref[...]

    def body(ci, carry):
        cc = nchunk - 1 - ci if reverse else ci
        r0 = pl.multiple_of(cc * CHUNK, CHUNK)
        rs = pl.ds(r0, CHUNK)
        seqs = []
        for bb in range(nb):
            q = None if mode == "state" else _silu(q_ref[bb, rs, :].astype(F32))
            seqs.append((f_ref[bb, rs, :].astype(F32), v_ref[bb, rs, :].astype(F32), q))
        outs = _hgrn_chunks(seqs, lb, tri, st_ref, reverse, mode != "state")
        if mode == "state":
            return carry
        for bb in range(nb):
            o = outs[bb]
            if mode == "fwd":
                o_ref[bb, rs, :] = o.astype(BF16)
                continue
            o = o + op_ref[bb, rs, :].astype(F32)
            nw = nw_ref[...]
            parts = []
            for h in range(HG_HEADS):
                oh = o[:, h * HG_DK:(h + 1) * HG_DK]
                ms = jnp.mean(oh * oh, -1, keepdims=True)
                parts.append(oh * lax.rsqrt(ms + LN_EPS) * nw)
            on = jnp.concatenate(parts, axis=1)
            gs_ref[bb, rs, :] = (on * _silu(og_ref[bb, rs, :].astype(F32))).astype(BF16)
        return carry

    lax.fori_loop(0, nchunk, body, 0)

    if mode == "state":
        @pl.when(j == pl.num_programs(1) - 1)
        def _():
            sfin_ref[...] = st_ref[...]
    elif mode == "rev_out":
        for bb in range(nb):
            y_ref[bb] = jnp.dot(gs_ref[bb], wa_ref[...], preferred_element_type=F32)


def _tri(n, reverse, block=None):
    block = n if block is None else block
    i = np.arange(n)
    same = (i[:, None] // block) == (i[None, :] // block)
    m = (i[None, :] >= i[:, None]) if reverse else (i[None, :] <= i[:, None])
    return jnp.asarray((m & same).astype(np.float32))


def _hgrn(proj, s0, lb, reverse, mode, tt, o_prev=None, norm_w=None, w_a=None):
    b, t, _ = proj.shape
    nt = t // tt
    hw = HG_WIDTH
    nb = HG_NB if b % HG_NB == 0 else 1
    grid = (b // nb, nt)

    def tmap(col):
        if reverse:
            return lambda bi, j: (bi, nt - 1 - j, col)
        return lambda bi, j: (bi, j, col)

    pspec = lambda col: pl.BlockSpec((nb, tt, hw), tmap(col))
    sspec = pl.BlockSpec((nb, HG_HEADS, HG_DK, HG_DK), lambda bi, j: (bi, 0, 0, 0))
    cspec = lambda shape: pl.BlockSpec(shape, lambda bi, j: (0,) * len(shape))
    fcol = PB_FB if reverse else PB_FF
    tri = _tri(CHUNK, reverse)
    kern = functools.partial(_hgrn_kernel, reverse=reverse, mode=mode, nchunk=tt // CHUNK)
    st_scr = pltpu.VMEM((nb, HG_HEADS, HG_DK, HG_DK), F32)
    if mode == "state":
        return pl.pallas_call(
            kern, grid=grid,
            in_specs=[pspec(fcol), pspec(PB_IV), sspec, cspec((1, hw)), cspec((CHUNK, CHUNK))],
            out_specs=sspec,
            out_shape=jax.ShapeDtypeStruct(s0.shape, F32),
            scratch_shapes=[st_scr],
            compiler_params=_cparams(("arbitrary", "arbitrary")),
            name="hgrn_state",
        )(proj, proj, s0, lb, tri)
    if mode == "fwd":
        return pl.pallas_call(
            kern, grid=grid,
            in_specs=[pspec(fcol), pspec(PB_IV), pspec(PB_Q), sspec, cspec((1, hw)), cspec((CHUNK, CHUNK))],
            out_specs=pl.BlockSpec((nb, tt, hw), tmap(0)),
            out_shape=jax.ShapeDtypeStruct((b, t, hw), BF16),
            scratch_shapes=[st_scr],
            compiler_params=_cparams(("arbitrary", "arbitrary")),
            name="hgrn_fwd",
        )(proj, proj, proj, s0, lb, tri)
    return pl.pallas_call(
        kern, grid=grid,
        in_specs=[pspec(fcol), pspec(PB_IV), pspec(PB_Q), pspec(PB_OG),
                  pl.BlockSpec((nb, tt, hw), tmap(0)), sspec, cspec((1, hw)), cspec((CHUNK, CHUNK)),
                  cspec((1, HG_DK)), cspec((hw, D_MODEL))],
        out_specs=pl.BlockSpec((nb, tt, D_MODEL), tmap(0)),
        out_shape=jax.ShapeDtypeStruct((b, t, D_MODEL), F32),
        scratch_shapes=[st_scr, pltpu.VMEM((nb, tt, hw), BF16)],
        compiler_params=_cparams(("arbitrary", "arbitrary")),
        name="hgrn_rev",
    )(proj, proj, proj, proj, o_prev, s0, lb, tri, norm_w, w_a)


def _conv_silu(xp, w, b):
    r = xp.shape[0]
    rio = lax.broadcasted_iota(I32, (r, 1), 0)
    acc = xp * w[MA_CONV // 2:MA_CONV // 2 + 1] + b
    for kk in range(MA_CONV):
        d = kk - MA_CONV // 2
        if d == 0:
            continue
        sh = pltpu.roll(xp, (-d) % r, axis=0)
        valid = jnp.logical_and(rio + d >= 0, rio + d < r)
        acc = acc + jnp.where(valid, sh, 0.0) * w[kk:kk + 1]
    return _silu(acc)


def _expand(x, e):
    hi = x.astype(BF16)
    lo = (x - hi.astype(F32)).astype(BF16)
    return (jnp.dot(hi, e, preferred_element_type=F32) + jnp.dot(lo, e, preferred_element_type=F32))


def _ssd_kernel(*refs, reverse, mode, nrows):
    if mode == "state":
        (x0_ref, x1_ref, b_ref, c_ref, dt_ref, h0_ref, cw_ref, cb_ref, dtb_ref, a_ref, e_ref,
         tri_ref, hfin_ref, hst_ref) = refs
    elif mode == "fwd":
        (x0_ref, x1_ref, b_ref, c_ref, dt_ref, h0_ref, cw_ref, cb_ref, dtb_ref, a_ref, e_ref,
         tri_ref, y_ref, xc_ref, hst_ref) = refs
    else:
        (x0_ref, x1_ref, b_ref, c_ref, dt_ref, h0_ref, cw_ref, cb_ref, dtb_ref, a_ref, e_ref,
         tri_ref, z0_ref, z1_ref, yf_ref, dsk_ref, nw_ref, wb_ref, y_ref, hst_ref, ys_ref) = refs
    j = pl.program_id(1)

    @pl.when(j == 0)
    def _():
        hst_ref[...] = h0_ref[0]

    cw = cw_ref[...]
    cb = cb_ref[...]
    need_y = mode != "state"
    pieces = []
    for pi, pref in enumerate((x0_ref, x1_ref, b_ref, c_ref)):
        if pi == 3 and not need_y:
            pieces.append(None)
            continue
        cs = slice(pi * 1024, (pi + 1) * 1024)
        if mode == "rev_out":
            pieces.append(pref[0, 0].astype(F32))
            continue
        pieces.append(_conv_silu(pref[0, 0].astype(F32), cw[:, cs], cb[:, cs]))
        if mode == "fwd":
            xc_ref[0, 0, :, cs] = pieces[-1].astype(BF16)
    xh = jnp.concatenate(pieces[:2], axis=1)
    bm = pieces[2].astype(BF16)
    cm = pieces[3].astype(BF16) if need_y else None

    x = dt_ref[0, 0] + dtb_ref[...]
    dt = jnp.maximum(x, 0.0) + jnp.log1p(jnp.exp(-jnp.abs(x)))
    la = dt * a_ref[...]
    e_mat = e_ref[...]
    tri = tri_ref[...]
    lane0 = MA_HEADS if reverse else 0
    rows = lax.broadcasted_iota(I32, (CHUNK, CHUNK), 0)
    cols = lax.broadcasted_iota(I32, (CHUNK, CHUNK), 1)
    mask = (cols >= rows) if reverse else (cols <= rows)
    lane128 = lax.broadcasted_iota(I32, (CHUNK, 128), 1)

    npair = nrows // 128
    for pbi in range(npair):
        pb = npair - 1 - pbi if reverse else pbi
        prow = slice(pb * 128, (pb + 1) * 128)
        lcum2 = _dot_hilo(tri.astype(BF16), la[prow])
        lcum_t = lcum2.T
        dt_t = dt[prow].T
        for ci in range(2):
            cc = 1 - ci if reverse else ci
            crow = slice(pb * 128 + cc * CHUNK, pb * 128 + (cc + 1) * CHUNK)
            ccol = slice(cc * CHUNK, (cc + 1) * CHUNK)
            lc = lcum2[ccol]
            e = 0 if reverse else CHUNK - 1
            llast = lc[e:e + 1]
            xh_c = xh[crow]
            xb_c = xh_c.astype(BF16)
            w_out = _expand(dt[crow] * jnp.exp(llast - lc), e_mat)
            xd = (xh_c * w_out).astype(BF16)
            expl = _expand(jnp.broadcast_to(jnp.exp(llast), (8, 128)), e_mat)[0:1]
            bc = bm[crow]
            if need_y:
                dec_in = _expand(jnp.exp(lc), e_mat)
                cc_ = cm[crow]
            ys = []
            for g in range(MA_GROUPS):
                gs = slice(g * MA_STATE, (g + 1) * MA_STATE)
                gw = slice(g * MA_GW, (g + 1) * MA_GW)
                hg = hst_ref[g]
                if need_y:
                    cbg = lax.dot_general(cc_[:, gs], bc[:, gs], NT_DIMS, preferred_element_type=F32)
                    pair_out = []
                    for jp in range(2):
                        ms = []
                        for jj in range(2):
                            ln = lane0 + 4 * g + 2 * jp + jj
                            diff = lc[:, ln:ln + 1] - lcum_t[ln:ln + 1, ccol]
                            lmat = jnp.exp(jnp.where(mask, diff, NEG))
                            ms.append(cbg * lmat * dt_t[ln:ln + 1, ccol])
                        lhs = jnp.concatenate(ms, axis=0).astype(BF16)
                        xs = xb_c[:, g * MA_GW + jp * 128:g * MA_GW + (jp + 1) * 128]
                        res = jnp.dot(lhs, xs, preferred_element_type=F32)
                        pair_out.append(jnp.where(lane128 < MA_HEAD_DIM, res[:CHUNK], res[CHUNK:]))
                    y_intra = jnp.concatenate(pair_out, axis=1)
                    y_inter = jnp.dot(cc_[:, gs], hg.astype(BF16), preferred_element_type=F32)
                    ys.append(y_intra + y_inter * dec_in[:, gw])
                upd = lax.dot_general(bc[:, gs], xd[:, gw], TN_DIMS, preferred_element_type=F32)
                hst_ref[g] = expl[:, gw] * hg + upd
            if not need_y:
                continue
            y = jnp.concatenate(ys, axis=1)
            if mode == "fwd":
                y_ref[0, 0, crow, :] = y.astype(BF16)
            else:
                ytot = yf_ref[0, 0, crow, :].astype(F32) + y + xh_c * dsk_ref[...]
                zc = jnp.concatenate([z0_ref[0, 0, crow, :], z1_ref[0, 0, crow, :]], axis=1).astype(F32)
                yz = ytot * _silu(zc)
                nw = nw_ref[...]
                parts = []
                for g in range(MA_GROUPS):
                    gw = slice(g * MA_GW, (g + 1) * MA_GW)
                    yg = yz[:, gw]
                    ms_ = jnp.mean(yg * yg, -1, keepdims=True)
                    parts.append(yg * lax.rsqrt(ms_ + LN_EPS) * nw[:, gw])
                ys_ref[crow, :] = jnp.concatenate(parts, axis=1).astype(BF16)

    if mode == "state":
        @pl.when(j == pl.num_programs(1) - 1)
        def _():
            hfin_ref[0] = hst_ref[...]
    elif mode == "rev_out":
        y_ref[0, 0] = jnp.dot(ys_ref[...], wb_ref[...], preferred_element_type=F32)


def _ssd(pcm, dtraw, h0, consts, reverse, mode, y_f=None, xconv=None, extra=None):
    b, ncols, nrows, _ = pcm.shape
    conv_w, conv_b, dt_bias, a128, e_mat = consts

    def wmap(col):
        if reverse:
            return lambda bi, j: (bi, ncols - 1 - j, 0, col)
        return lambda bi, j: (bi, j, 0, col)

    pspec = lambda col: pl.BlockSpec((1, 1, nrows, 1024), wmap(col))
    dspec = pl.BlockSpec((1, 1, nrows, DT_W), wmap(0))
    hspec = pl.BlockSpec((1, MA_GROUPS, MA_STATE, MA_GW), lambda bi, j: (bi, 0, 0, 0))
    cspec = lambda shape: pl.BlockSpec(shape, lambda bi, j: (0,) * len(shape))
    tri = _tri(128, reverse, block=CHUNK)
    kern = functools.partial(_ssd_kernel, reverse=reverse, mode=mode, nrows=nrows)
    base_specs = [pspec(PC_X0), pspec(PC_X1), pspec(PC_B), pspec(PC_C), dspec, hspec,
                  cspec((MA_CONV, 4096)), cspec((1, 4096)), cspec((1, DT_W)), cspec((1, DT_W)),
                  cspec((DT_W, MA_INNER)), cspec((128, 128))]
    xsrc = pcm if xconv is None else xconv
    base_args = [xsrc, xsrc, xsrc, xsrc, dtraw, h0, conv_w, conv_b, dt_bias, a128, e_mat, tri]
    h_scr = pltpu.VMEM((MA_GROUPS, MA_STATE, MA_GW), F32)
    sem = _cparams(("arbitrary", "arbitrary"))
    if mode == "state":
        return pl.pallas_call(
            kern, grid=(b, ncols), in_specs=base_specs, out_specs=hspec,
            out_shape=jax.ShapeDtypeStruct(h0.shape, F32), scratch_shapes=[h_scr],
            compiler_params=sem, name="ssd_state")(*base_args)
    if mode == "fwd":
        nconv = MA_INNER + 2 * MA_GROUPS * MA_STATE
        return pl.pallas_call(
            kern, grid=(b, ncols), in_specs=base_specs,
            out_specs=[pl.BlockSpec((1, 1, nrows, MA_INNER), wmap(0)),
                       pl.BlockSpec((1, 1, nrows, nconv), wmap(0))],
            out_shape=[jax.ShapeDtypeStruct((b, ncols, nrows, MA_INNER), BF16),
                       jax.ShapeDtypeStruct((b, ncols, nrows, nconv), BF16)],
            scratch_shapes=[h_scr], compiler_params=sem, name="ssd_fwd")(*base_args)
    d_skip, norm_w, w_b = extra
    return pl.pallas_call(
        kern, grid=(b, ncols),
        in_specs=base_specs + [pspec(PC_Z0), pspec(PC_Z1),
                               pl.BlockSpec((1, 1, nrows, MA_INNER), wmap(0)),
                               cspec((1, MA_INNER)), cspec((1, MA_INNER)), cspec((MA_INNER, D_MODEL))],
        out_specs=pl.BlockSpec((1, 1, nrows, D_MODEL), wmap(0)),
        out_shape=jax.ShapeDtypeStruct((b, ncols, nrows, D_MODEL), F32),
        scratch_shapes=[h_scr, pltpu.VMEM((nrows, MA_INNER), BF16)],
        compiler_params=sem, name="ssd_rev")(*base_args, pcm, pcm, y_f, d_skip, norm_w, w_b)


def _merge_kernel(x_ref, ya_ref, yb_ref, perm_ref, ga_ref, gb_ref, g1_ref, wo_ref, lg_ref, lb_ref, o_ref):
    tm, d = x_ref.shape
    yb = yb_ref[...].reshape(tm, d)
    hi = yb.astype(BF16)
    lo = (yb - hi.astype(F32)).astype(BF16)
    perm = perm_ref[...]
    yb = (jnp.dot(perm, hi, preferred_element_type=F32) + jnp.dot(perm, lo, preferred_element_type=F32))
    m = (_sigmoid(ga_ref[...].astype(F32)) * ya_ref[...] + _sigmoid(gb_ref[...].astype(F32)) * yb)
    out = jnp.dot(m.astype(BF16), wo_ref[...], preferred_element_type=F32)
    z = DN_ALPHA * x_ref[...] + g1_ref[0] * out
    o_ref[...] = _layer_norm(z) * lg_ref[...] + lb_ref[...]


def _merge(x2, y_a, y_b_cm, proj_rm, g1, w_out, ln_g, ln_b, tokens_per_batch):
    m, d = x2.shape
    ncols = y_b_cm.shape[1]
    tm = COL_TILE * ncols
    tpb = tokens_per_batch // tm
    tspec = pl.BlockSpec((tm, d), lambda i: (i, 0))
    cspec = lambda shape: pl.BlockSpec(shape, lambda i: (0,) * len(shape))
    perm = jnp.asarray(_perm_matrix(ncols, COL_TILE)).astype(BF16)
    return pl.pallas_call(
        _merge_kernel, grid=(m // tm,),
        in_specs=[tspec, tspec,
                  pl.BlockSpec((1, ncols, COL_TILE, d), lambda i: (i // tpb, 0, i % tpb, 0)),
                  cspec((tm, tm)),
                  pl.BlockSpec((tm, d), lambda i: (i, PB_GA)),
                  pl.BlockSpec((tm, d), lambda i: (i, PB_GB)),
                  pl.BlockSpec((1, 1, d), lambda i: (i // tpb, 0, 0)),
                  cspec((d, d)), cspec((1, d)), cspec((1, d))],
        out_specs=tspec,
        out_shape=jax.ShapeDtypeStruct((m, d), F32),
        compiler_params=_cparams(("arbitrary",)),
        name="merge",
    )(x2, y_a, y_b_cm, perm, proj_rm, proj_rm, g1, w_out, ln_g, ln_b)


KEY_BIAS = 0x40000000
KEY_MIN = 1.1754944e-38
LOW_MASK = PEER_NKEYS - 1


def _sort_pairs(n):
    def merge(lo, hi, r):
        step = r * 2
        if step < hi - lo:
            yield from merge(lo, hi, step)
            yield from merge(lo + r, hi, step)
            for i in range(lo + r, hi - r, step):
                yield (i, i + r)
        else:
            yield (lo, lo + r)

    def sort(lo, hi):
        if hi - lo >= 1:
            mid = lo + (hi - lo) // 2
            yield from sort(lo, mid)
            yield from sort(mid + 1, hi)
            yield from merge(lo, hi, 1)

    return list(sort(0, n - 1))


def _merge_pairs(n):
    out, j = [], n // 2
    while j:
        out += [(i, i | j) for i in range(n) if not i & j]
        j //= 2
    return out


def _cex(v, pairs):
    v = list(v)
    for i, j in pairs:
        v[i], v[j] = jnp.maximum(v[i], v[j]), jnp.minimum(v[i], v[j])
    return v


def _cex_p(v, c, pairs):
    v, c = list(v), list(c)
    for i, j in pairs:
        m = v[i] >= v[j]
        v[i], v[j], c[i], c[j] = (jnp.maximum(v[i], v[j]), jnp.minimum(v[i], v[j]),
                                  jnp.where(m, c[i], c[j]), jnp.where(m, c[j], c[i]))
    return v, c


def _to_key(s, low):
    b = pltpu.bitcast(s, I32)
    k = jnp.where(b < 0, jnp.bitwise_xor(b, 0x7FFFFFFF), b)
    k = jnp.right_shift(k, 1) + KEY_BIAS
    k = jnp.bitwise_or(jnp.bitwise_and(k, ~LOW_MASK), low)
    return pltpu.bitcast(k, F32)


def _from_key(kf):
    k = pltpu.bitcast(kf, I32)
    low = jnp.bitwise_and(k, LOW_MASK)
    k = jnp.left_shift(jnp.bitwise_and(k, ~LOW_MASK) - KEY_BIAS, 1)
    b = jnp.where(k < 0, jnp.bitwise_xor(k, 0x7FFFFFFF), k)
    return pltpu.bitcast(b, F32), low


def _top16_keys(st, sub):
    v = []
    for r in range(PEER_NKEYS // 8):
        v.append(_to_key(st[8 * r:8 * r + 8], (LOW_MASK - 8 * r) - sub))
    v = _cex(v, _sort_pairs(16))
    for sh in (4, 2, 1):
        part = [pltpu.roll(x, sh, axis=0) for x in v]
        v = _cex([jnp.maximum(v[r], part[15 - r]) for r in range(16)], _merge_pairs(16))
    return v


def _natural(lst, half, sub):
    x = lst[8 * half]
    for s_ in range(1, 8):
        x = jnp.where(sub == s_, lst[8 * half + s_], x)
    return x


def _slot_table():
    slots = {}
    for s_ in range(8):
        slots[(0, s_)] = (0, s_)
        slots[(1, s_)] = (0, 8 + s_)
        slots[(3, s_)] = (8 + s_, 0)
    for s_ in range(1, 8):
        slots[(2, s_)] = (s_, 0)
        slots[(4, s_)] = (1, s_)
    for s_ in range(2, 8):
        slots[(5, s_)] = (s_, 1)
    for s_ in (2, 3, 4):
        slots[(6, s_)] = (2, s_)
    for s_ in (2, 3):
        slots[(7, s_)] = (3, s_)
    slots[(7, 4)] = (4, 2)
    want = sorted((j, l) for j in range(PEER_TOPK) for l in range(PEER_TOPK) if (j + 1) * (l + 1) <= PEER_TOPK)
    assert sorted(slots.values()) == want
    low = -np.ones((8, 8), np.int32)
    for (q, s_), jl in slots.items():
        low[q, s_] = LOW_MASK - want.index(jl)
    return jnp.asarray(np.repeat(low.reshape(64, 1), 128, axis=1))


def _retrieve_head(q_ref, sk_ref, slot_low, sub, t0, h):
    vals, idxs = [], []
    for i in range(2):
        cidx = 2 * h + i
        qc = q_ref[pl.ds(t0, 128), cidx * PEER_HALF:(cidx + 1) * PEER_HALF]
        st = lax.dot_general(sk_ref[cidx], qc, NT_DIMS, preferred_element_type=F32)
        dec = [_from_key(k) for k in _top16_keys(st, sub)]
        vals.append([d[0] for d in dec])
        idxs.append([LOW_MASK - d[1] for d in dec])
    v1, v2 = vals
    i1, i2 = idxs
    s1n = [_natural(v1, hf, sub) for hf in range(2)]
    s2n = [_natural(v2, hf, sub) for hf in range(2)]
    i1n = [_natural(i1, hf, sub) * PEER_NKEYS for hf in range(2)]
    i2n = [_natural(i2, hf, sub) for hf in range(2)]
    row4 = sub == 4
    cand = [(v1[0] + s2n[0], i1[0] * PEER_NKEYS + i2n[0]),
            (v1[0] + s2n[1], i1[0] * PEER_NKEYS + i2n[1]),
            (s1n[0] + v2[0], i1n[0] + i2[0]),
            (s1n[1] + v2[0], i1n[1] + i2[0]),
            (v1[1] + s2n[0], i1[1] * PEER_NKEYS + i2n[0]),
            (s1n[0] + v2[1], i1n[0] + i2[1]),
            (v1[2] + s2n[0], i1[2] * PEER_NKEYS + i2n[0]),
            (jnp.where(row4, s1n[0] + v2[2], v1[3] + s2n[0]),
             jnp.where(row4, i1n[0] + i2[2], i1[3] * PEER_NKEYS + i2n[0]))]
    keys, codes = [], []
    for qi, (cv, cc) in enumerate(cand):
        low = slot_low[8 * qi:8 * qi + 8]
        keys.append(jnp.where(low >= 0, _to_key(cv, low), KEY_MIN))
        codes.append(cc)
    keys, codes = _cex_p(keys, codes, _sort_pairs(8))
    pk = [pltpu.roll(x, 4, axis=0) for x in keys]
    pc = [pltpu.roll(x, 4, axis=0) for x in codes]
    keys, codes = _cex_p(keys + pk[::-1], codes + pc[::-1], _merge_pairs(16))
    for sh in (2, 1):
        pk = [pltpu.roll(x, sh, axis=0) for x in keys]
        pc = [pltpu.roll(x, sh, axis=0) for x in codes]
        mk, mc = [], []
        for r in range(16):
            m = keys[r] >= pk[15 - r]
            mk.append(jnp.maximum(keys[r], pk[15 - r]))
            mc.append(jnp.where(m, codes[r], pc[15 - r]))
        keys, codes = _cex_p(mk, mc, _merge_pairs(16))
    tv = [_from_key(k)[0] for k in keys]
    ex = [jnp.exp(t - tv[0]) for t in tv]
    den = ex[0]
    for e_ in ex[1:]:
        den = den + e_
    inv = 1.0 / den
    return codes, [e_ * inv for e_ in ex]


def _peer_kernel(x_ref, sh_ref, sc_ref, g2_ref, wq_ref, sk_ref, slot_ref, u_ref, v_ref, lg_ref, lb_ref,
                 o_ref, xm_ref, q_ref, acc_ref, t_ref, tc_ref, tg_ref, ai_ref, bi_ref, gt_ref):
    s = pl.program_id(1)
    p = x_ref.shape[0]

    @pl.when(s == 0)
    def _():
        xm = (_layer_norm(x_ref[...]) * (1.0 + sc_ref[0]) + sh_ref[0]).astype(BF16)
        xm_ref[...] = xm
        acc_ref[...] = jnp.zeros_like(acc_ref)
        q_ref[...] = jnp.dot(xm, wq_ref[...], preferred_element_type=F32).astype(BF16)
        slot_low = slot_ref[...]
        sub = lax.broadcasted_iota(I32, (8, 128), 0)

        def tile(ti, carry):
            t0 = pl.multiple_of(ti * 128, 128)
            for h in range(PEER_HEADS):
                codes, gates = _retrieve_head(q_ref, sk_ref, slot_low, sub, t0, h)
                for r in range(PEER_TOPK):
                    row = h * PEER_TOPK + r
                    tc_ref[row:row + 1, :] = codes[r][0:1].astype(F32)
                    tg_ref[row:row + 1, :] = gates[r][0:1]
            ct = tc_ref[...].T.astype(I32)
            ai_ref[pl.ds(t0, 128), :] = jnp.right_shift(ct, 7).astype(F32)
            bi_ref[pl.ds(t0, 128), :] = jnp.bitwise_and(ct, LOW_MASK).astype(F32)
            gt_ref[pl.ds(t0, 128), :] = tg_ref[...].T
            return carry

        lax.fori_loop(0, p // 128, tile, 0)

        sub128 = lax.broadcasted_iota(I32, (PEER_NKEYS, 128), 0)
        amap = jnp.bitwise_or(jnp.left_shift(jnp.bitwise_and(sub128, 63), 1),
                              jnp.right_shift(sub128, 6)).astype(F32)
        bmap = sub128.astype(F32)

        def tok8(i, carry):
            t0 = pl.multiple_of(i * TOK_UNROLL, TOK_UNROLL)
            a8 = ai_ref[pl.ds(t0, TOK_UNROLL), :]
            b8 = bi_ref[pl.ds(t0, TOK_UNROLL), :]
            g8 = gt_ref[pl.ds(t0, TOK_UNROLL), :]
            for j in range(TOK_UNROLL):
                at = jnp.where(a8[j:j + 1] == amap, 1.0, 0.0).astype(BF16)
                rt = jnp.where(b8[j:j + 1] == bmap, g8[j:j + 1], 0.0).astype(BF16)
                w = lax.dot_general(at, rt, NT_DIMS, preferred_element_type=F32)
                lo = pltpu.bitcast(w[:64], I32) + 0x8000
                hi = pltpu.bitcast(w[64:], I32) + 0x8000
                lo = lax.shift_right_logical(lo, 16)
                hi = jnp.bitwise_and(hi, -65536)
                r0 = pl.multiple_of((t0 + j) * PEER_PITCH, 8)
                t_ref[pl.ds(r0, 64), :] = jnp.bitwise_or(lo, hi)
            return carry

        lax.fori_loop(0, p // TOK_UNROLL, tok8, 0)

    hval = lax.dot_general(xm_ref[...], u_ref[...], NT_DIMS, preferred_element_type=F32)
    act = 0.5 * hval * (1.0 + lax.erf(hval * 0.7071067811865476))
    wgts = []
    for g in range(PEER_G):
        wp = t_ref[pl.ds(s * PEER_G + g, p, stride=PEER_PITCH), :]
        wgts.append(pltpu.bitcast(jnp.left_shift(wp, 16), F32))
        wgts.append(pltpu.bitcast(jnp.bitwise_and(wp, -65536), F32))
    z = (jnp.concatenate(wgts, axis=1) * act).astype(BF16)
    acc_ref[...] += jnp.dot(z, v_ref[...], preferred_element_type=F32)

    @pl.when(s == pl.num_programs(1) - 1)
    def _():
        z = DN_ALPHA * x_ref[...] + g2_ref[0] * acc_ref[...]
        o_ref[...] = _layer_norm(z) * lg_ref[...] + lb_ref[...]


def _peer(x1, shift, scale, g2, wq, sk, u2, v2, ln_g, ln_b, tokens_per_batch):
    m, d = x1.shape
    p = PEER_P
    tpb = tokens_per_batch // p
    blk = PEER_G * 2 * PEER_NKEYS
    nsteps = u2.shape[0] // blk
    slot = _slot_table()
    cspec = lambda shape: pl.BlockSpec(shape, lambda i, s: (0,) * len(shape))
    bspec = pl.BlockSpec((1, 1, d), lambda i, s: (i // tpb, 0, 0))
    nent = PEER_HEADS * PEER_TOPK
    return pl.pallas_call(
        _peer_kernel, grid=(m // p, nsteps),
        in_specs=[pl.BlockSpec((p, d), lambda i, s: (i, 0)), bspec, bspec, bspec,
                  cspec(wq.shape), cspec(sk.shape), cspec(slot.shape),
                  pl.BlockSpec((blk, d), lambda i, s: (s, 0)),
                  pl.BlockSpec((blk, d), lambda i, s: (s, 0)),
                  cspec((1, d)), cspec((1, d))],
        out_specs=pl.BlockSpec((p, d), lambda i, s: (i, 0)),
        out_shape=jax.ShapeDtypeStruct((m, d), F32),
        scratch_shapes=[pltpu.VMEM((p, d), BF16), pltpu.VMEM((p, 2 * PEER_HEADS * PEER_HALF), BF16),
                        pltpu.VMEM((p, d), F32), pltpu.VMEM((p * PEER_PITCH, 128), I32),
                        pltpu.VMEM((nent, 128), F32), pltpu.VMEM((nent, 128), F32),
                        pltpu.VMEM((p, nent), F32), pltpu.VMEM((p, nent), F32), pltpu.VMEM((p, nent), F32)],
        compiler_params=_cparams(("arbitrary", "arbitrary")),
        name="peer",
    )(x1, shift, scale, g2, wq, sk, slot, u2, v2, ln_g, ln_b)


def _layer(x, c, ctx, c_ctx, lw, lb_l):
    b, t, d = x.shape
    tc = ctx.shape[1]

    npad = -(-(b + 1) // 8) * 8
    c_all = jnp.zeros((npad, d), F32).at[:b].set(c).at[b].set(c_ctx)
    mod = _ada(c_all, lw["w_ada"], lw["b_ada"])
    sh1, sc1, g1, sh2, sc2, g2 = [mod[:b, i * d:(i + 1) * d].reshape(b, 1, d) for i in range(6)]
    csh1 = jnp.broadcast_to(mod[b, 0:d], (b, 1, d))
    csc1 = jnp.broadcast_to(mod[b, d:2 * d], (b, 1, d))

    w_in = lw["w_in"]
    off = np.cumsum([0, HG_WIDTH, HG_WIDTH, HG_WIDTH, MA_INNER + MA_GROUPS * MA_STATE, MA_HEADS, MA_HEADS,
                     HG_WIDTH, HG_WIDTH, MA_GROUPS * MA_STATE, MA_INNER, D_MODEL, D_MODEL])
    o_ff, o_fb, o_iv, o_xb, o_dtf, o_dtb, o_q, o_og, o_cm, o_z, o_ga, o_gb = off[:12]
    col = lambda o, n: w_in[:, o:o + n]
    w_rm = jnp.concatenate([col(o_ff, 1024), col(o_fb, 1024), col(o_iv, 1024), col(o_q, 1024),
                            col(o_og, 1024), col(o_ga, 1024), col(o_gb, 1024)], axis=1).astype(BF16)
    w_cm = jnp.concatenate([col(o_xb, 3072), col(o_cm, 1024), col(o_z, 2048)], axis=1).astype(BF16)
    w_dt = jnp.concatenate([col(o_dtf, 2 * MA_HEADS), jnp.zeros((d, DT_W - 2 * MA_HEADS), F32)],
                           axis=1).astype(BF16)
    lb_f = lb_l[0].reshape(1, HG_WIDTH)
    lb_b = lb_l[1].reshape(1, HG_WIDTH)
    zpad = jnp.zeros((DT_W - 2 * MA_HEADS,), F32)
    dt_bias = jnp.concatenate([lw["ma_dt_bias"].reshape(-1), zpad]).reshape(1, DT_W)
    a128 = jnp.concatenate([-jnp.exp(lw["ma_a_log"].astype(F32)).reshape(-1), zpad]).reshape(1, DT_W)
    conv_w = lw["ma_conv_w"].T
    conv_b = lw["ma_conv_b"].reshape(1, -1)
    head_of = np.arange(MA_INNER) // MA_HEAD_DIM
    e_f = jnp.asarray((np.arange(DT_W)[:, None] == head_of[None, :]).astype(np.float32)).astype(BF16)
    e_b = jnp.asarray((np.arange(DT_W)[:, None] == (head_of + MA_HEADS)[None, :]).astype(np.float32)).astype(BF16)
    consts_f = (conv_w, conv_b, dt_bias, a128, e_f)
    consts_b = (conv_w, conv_b, dt_bias, a128, e_b)
    d_skip = jnp.repeat(lw["ma_d"], MA_HEAD_DIM).reshape(1, MA_INNER)
    ma_norm_w = lw["ma_norm_w"].reshape(1, MA_INNER)
    hg_norm_w = lw["hg_norm_w"].reshape(1, HG_DK)
    w_a = lw["w_branch_a"].astype(BF16)
    w_b = lw["w_branch_b"].astype(BF16)
    w_o = lw["w_out"].astype(BF16)

    projc = _inproj_rm(ctx.reshape(b * tc, d), csh1, csc1, w_rm, tc, tc).reshape(b, tc, NP_RM)
    pcmc, dtc = _inproj_cm(ctx, csh1, csc1, w_cm, w_dt, 1)
    zs = jnp.zeros((b, HG_HEADS, HG_DK, HG_DK), F32)
    zh = jnp.zeros((b, MA_GROUPS, MA_STATE, MA_GW), F32)
    s_hf = _hgrn(projc, zs, lb_f, False, "state", tc)
    s_hb = _hgrn(projc, zs, lb_b, True, "state", tc)
    h_f = _ssd(pcmc, dtc, zh, consts_f, False, "state")
    h_b = _ssd(pcmc, dtc, zh, consts_b, True, "state")

    x2 = x.reshape(b * t, d)
    proj = _inproj_rm(x2, sh1, sc1, w_rm, t, min(1024, t))
    pcm, dtr = _inproj_cm(x, sh1, sc1, w_cm, w_dt, GRID_W)
    proj3 = proj.reshape(b, t, NP_RM)
    tt = min(512, t)
    o_f = _hgrn(proj3, s_hf, lb_f, False, "fwd", tt)
    y_a = _hgrn(proj3, s_hb, lb_b, True, "rev_out", tt, o_prev=o_f, norm_w=hg_norm_w, w_a=w_a)
    y_f, xconv = _ssd(pcm, dtr, h_f, consts_f, False, "fwd")
    y_b = _ssd(pcm, dtr, h_b, consts_b, True, "rev_out", y_f=y_f, xconv=xconv,
               extra=(d_skip, ma_norm_w, w_b))
    x1 = _merge(x2, y_a.reshape(b * t, d), y_b, proj, g1, w_o,
                lw["ln1_g"].reshape(1, d), lw["ln1_b"].reshape(1, d), t)

    wq = lw["peer_wq"].astype(BF16)
    sk = lw["peer_subkeys"].reshape(PEER_HEADS * 2, PEER_NKEYS, PEER_HALF).astype(BF16)
    u2 = lw["peer_u"].astype(BF16)
    v2 = lw["peer_v"].astype(BF16)
    out = _peer(x1, sh2, sc2, g2, wq, sk, u2, v2, lw["ln2_g"].reshape(1, d), lw["ln2_b"].reshape(1, d), t)
    return out.reshape(b, t, d)


def kernel(x, c, ctx, c_ctx, w_ada, b_ada, w_in, hg_lb_logits, hg_norm_w, ma_conv_w, ma_conv_b, ma_dt_bias,
           ma_a_log, ma_d, ma_norm_w, w_branch_a, w_branch_b, w_out, ln1_g, ln1_b, peer_wq, peer_subkeys,
           peer_u, peer_v, ln2_g, ln2_b):
    depth = w_in.shape[0]
    assert depth == 1, "the scan states of a single (last) layer are implemented"
    lb_all = jnp.cumsum(jax.nn.softmax(hg_lb_logits.astype(F32), axis=1), axis=1)
    lw = {"w_ada": w_ada[0], "b_ada": b_ada[0], "w_in": w_in[0], "hg_norm_w": hg_norm_w[0],
          "ma_conv_w": ma_conv_w[0], "ma_conv_b": ma_conv_b[0], "ma_dt_bias": ma_dt_bias[0],
          "ma_a_log": ma_a_log[0], "ma_d": ma_d[0], "ma_norm_w": ma_norm_w[0],
          "w_branch_a": w_branch_a[0], "w_branch_b": w_branch_b[0], "w_out": w_out[0],
          "ln1_g": ln1_g[0], "ln1_b": ln1_b[0], "peer_wq": peer_wq[0], "peer_subkeys": peer_subkeys[0],
          "peer_u": peer_u[0], "peer_v": peer_v[0], "ln2_g": ln2_g[0], "ln2_b": ln2_b[0]}
    return _layer(x, c, ctx, c_ctx, lw, lb_all[:, 0])
```

```python
import functools

import numpy as np
import jax
import jax.numpy as jnp
from jax import lax
from jax.experimental import pallas as pl
from jax.experimental.pallas import tpu as pltpu

F32 = jnp.float32
BF16 = jnp.bfloat16
I32 = jnp.int32
HI = lax.Precision.HIGHEST

D_MODEL = 1024
GRID_W = 64
CHUNK = 64
SUB = 16
HG_NB = 2
HG_HEADS = 8
HG_DK = 128
HG_WIDTH = HG_HEADS * HG_DK
MA_INNER = 2 * D_MODEL
MA_HEAD_DIM = 64
MA_HEADS = MA_INNER // MA_HEAD_DIM
MA_GROUPS = 8
MA_STATE = 128
MA_GW = MA_INNER // MA_GROUPS
MA_CONV = 5
PEER_HEADS = 8
PEER_NKEYS = 128
PEER_TOPK = 16
PEER_HALF = 128
DN_ALPHA = 2.0 ** 0.25
LN_EPS = 1e-6
NEG = -3.0e38
EXP_CLAMP = 60.0

PB_FF, PB_FB, PB_IV, PB_Q, PB_OG, PB_GA, PB_GB = range(7)
PC_X0, PC_X1, PC_B, PC_C, PC_Z0, PC_Z1 = range(6)
NP_RM = 7 * 1024
NP_CM = 6 * 1024
DT_W = 128
COL_TILE = 8

PEER_P = 512
PEER_G = 4
PEER_PITCH = 72
TOK_UNROLL = 32
VMEM_LIMIT = 56 * 1024 * 1024

NT_DIMS = (((1,), (1,)), ((), ()))
TN_DIMS = (((0,), (0,)), ((), ()))


def _cparams(sem):
    return pltpu.CompilerParams(dimension_semantics=sem, vmem_limit_bytes=VMEM_LIMIT)


def _sigmoid(x):
    return jax.nn.sigmoid(x)


def _silu(x):
    return x * jax.nn.sigmoid(x)


def _dot_hilo(m, x):
    hi = x.astype(BF16)
    lo = (x - hi.astype(F32)).astype(BF16)
    return jnp.dot(m, hi, preferred_element_type=F32) + jnp.dot(m, lo, preferred_element_type=F32)


def _layer_norm(x):
    mu = jnp.mean(x, -1, keepdims=True)
    xc = x - mu
    var = jnp.mean(xc * xc, -1, keepdims=True)
    return xc * lax.rsqrt(var + LN_EPS)


def _ada_kernel(c_ref, w_ref, b_ref, o_ref):
    c = c_ref[...]
    o_ref[...] = jnp.dot(_silu(c), w_ref[...], precision=HI, preferred_element_type=F32) + b_ref[...]


def _ada(c_all, w_ada, b_ada):
    n, d = c_all.shape
    e = w_ada.shape[1]
    tn = 1024
    return pl.pallas_call(
        _ada_kernel,
        grid=(e // tn,),
        in_specs=[pl.BlockSpec((n, d), lambda j: (0, 0)),
                  pl.BlockSpec((d, tn), lambda j: (0, j)),
                  pl.BlockSpec((1, tn), lambda j: (0, j))],
        out_specs=pl.BlockSpec((n, tn), lambda j: (0, j)),
        out_shape=jax.ShapeDtypeStruct((n, e), F32),
        compiler_params=_cparams(("arbitrary",)),
        name="ada",
    )(c_all, w_ada, b_ada.reshape(1, e))


def _inproj_rm_kernel(x_ref, sh_ref, sc_ref, w_ref, o_ref, u_ref):
    @pl.when(pl.program_id(1) == 0)
    def _():
        u = _layer_norm(x_ref[...]) * (1.0 + sc_ref[0]) + sh_ref[0]
        u_ref[...] = u.astype(BF16)

    o_ref[...] = jnp.dot(u_ref[...], w_ref[...], preferred_element_type=F32).astype(BF16)


def _inproj_rm(x2, shift, scale, w_rm, tokens_per_batch, tm):
    m, d = x2.shape
    tpb = tokens_per_batch // tm
    tn = NP_RM // 2
    return pl.pallas_call(
        _inproj_rm_kernel,
        grid=(m // tm, NP_RM // tn),
        in_specs=[pl.BlockSpec((tm, d), lambda i, j: (i, 0)),
                  pl.BlockSpec((1, 1, d), lambda i, j: (i // tpb, 0, 0)),
                  pl.BlockSpec((1, 1, d), lambda i, j: (i // tpb, 0, 0)),
                  pl.BlockSpec((d, tn), lambda i, j: (0, j))],
        out_specs=pl.BlockSpec((tm, tn), lambda i, j: (i, j)),
        out_shape=jax.ShapeDtypeStruct((m, NP_RM), BF16),
        scratch_shapes=[pltpu.VMEM((tm, d), BF16)],
        compiler_params=_cparams(("arbitrary", "arbitrary")),
        name="inproj_rm",
    )(x2, shift, scale, w_rm)


def _inproj_cm_kernel(*refs, permute):
    if permute:
        x_ref, sh_ref, sc_ref, perm_ref, w_ref, wdt_ref, o_ref, dt_ref, u_ref = refs
    else:
        x_ref, sh_ref, sc_ref, w_ref, wdt_ref, o_ref, dt_ref, u_ref = refs
    tm, d = u_ref.shape

    @pl.when(pl.program_id(2) == 0)
    def _():
        x = x_ref[...].reshape(tm, d)
        ub = (_layer_norm(x) * (1.0 + sc_ref[0]) + sh_ref[0]).astype(BF16)
        if permute:
            ub = jnp.dot(perm_ref[...], ub, preferred_element_type=F32).astype(BF16)
        u_ref[...] = ub
        dt_ref[...] = jnp.dot(ub, wdt_ref[...], preferred_element_type=F32).reshape(dt_ref.shape)

    o_ref[...] = jnp.dot(u_ref[...], w_ref[...], preferred_element_type=F32).astype(BF16).reshape(o_ref.shape)


def _perm_matrix(nrows, ncols):
    n = nrows * ncols
    p = np.zeros((n, n), np.float32)
    r, w = np.meshgrid(np.arange(nrows), np.arange(ncols), indexing="ij")
    p[(w * nrows + r).ravel(), (r * ncols + w).ravel()] = 1.0
    return p


def _inproj_cm(x, shift, scale, w_cm, w_dt, ncols):
    b, t, d = x.shape
    rows = t // ncols
    tn = NP_CM // 3
    cspec = lambda shape: pl.BlockSpec(shape, lambda bi, wi, j: (0,) * len(shape))
    mspec = pl.BlockSpec((1, 1, d), lambda bi, wi, j: (bi, 0, 0))
    if ncols == 1:
        ct, tm = 1, rows
        xv = x
        xspec = pl.BlockSpec((1, rows, d), lambda bi, wi, j: (bi, 0, 0))
        extra, especs = [], []
    else:
        ct, tm = COL_TILE, rows * COL_TILE
        xv = x.reshape(b, rows, ncols, d)
        xspec = pl.BlockSpec((1, rows, ct, d), lambda bi, wi, j: (bi, 0, wi, 0))
        extra = [jnp.asarray(_perm_matrix(rows, ct)).astype(BF16)]
        especs = [cspec((tm, tm))]
    return pl.pallas_call(
        functools.partial(_inproj_cm_kernel, permute=ncols > 1),
        grid=(b, ncols // ct, NP_CM // tn),
        in_specs=[xspec, mspec, mspec] + especs +
                 [pl.BlockSpec((d, tn), lambda bi, wi, j: (0, j)), cspec((d, DT_W))],
        out_specs=[pl.BlockSpec((1, ct, rows, tn), lambda bi, wi, j: (bi, wi, 0, j)),
                   pl.BlockSpec((1, ct, rows, DT_W), lambda bi, wi, j: (bi, wi, 0, 0))],
        out_shape=[jax.ShapeDtypeStruct((b, ncols, rows, NP_CM), BF16),
                   jax.ShapeDtypeStruct((b, ncols, rows, DT_W), F32)],
        scratch_shapes=[pltpu.VMEM((tm, d), BF16)],
        compiler_params=_cparams(("arbitrary", "arbitrary", "arbitrary")),
        name="inproj_cm",
    )(xv, shift, scale, *extra, w_cm, w_dt)


def _hgrn_chunks(seqs, lb, tri, st_ref, reverse, need_o):
    c = seqs[0][0].shape[0]
    heads = [(bb, h) for bb in range(len(seqs)) for h in range(HG_HEADS)]
    sl = lambda h: slice(h * HG_DK, (h + 1) * HG_DK)
    rows = lax.broadcasted_iota(I32, (c, c), 0)
    cols = lax.broadcasted_iota(I32, (c, c), 1)
    mask = (cols >= rows) if reverse else (cols <= rows)
    tri_b = tri.astype(BF16)
    pre = []
    for fraw, v, q in seqs:
        fg = lb + (1.0 - lb) * _sigmoid(fraw)
        logf = jnp.log(fg)
        pre.append((1.0 - fg, logf, _dot_hilo(tri_b, logf), v.astype(BF16), q))
    e = 0 if reverse else c - 1
    kst, dec, qin, qcat, kcat, vbs = [], [], [], [], [], []
    for k, logf, cum, vb, q in pre:
        glast = cum[e:e + 1]
        kst.append((k * jnp.exp(glast - cum)).astype(BF16))
        dec.append(jnp.exp(glast))
        vbs.append(vb)
        if not need_o:
            continue
        qin.append((q * jnp.exp(cum)).astype(BF16))
        qhat, khat = [], []
        for i in range(c // SUB):
            ei = i * SUB + SUB - 1 if reverse else i * SUB
            ref = cum[ei:ei + 1] - logf[ei:ei + 1]
            blk = slice(i * SUB, (i + 1) * SUB)
            qh = (q[blk] * jnp.exp(cum[blk] - ref)).astype(BF16)
            pads = [jnp.zeros((SUB, qh.shape[1]), BF16)] * (c // SUB)
            pads[i] = qh
            qhat.append(jnp.concatenate(pads, axis=0))
            khat.append((k * jnp.exp(jnp.minimum(ref - cum, EXP_CLAMP))).astype(BF16))
        qcat.append(qhat)
        kcat.append(khat)
    att, inter = {}, {}
    for bb, h in heads:
        st = st_ref[bb, h]
        if need_o:
            inter[bb, h] = lax.dot_general(qin[bb][:, sl(h)], st.astype(BF16), NT_DIMS,
                                           preferred_element_type=F32)
            att[bb, h] = lax.dot_general(jnp.concatenate([x_[:, sl(h)] for x_ in qcat[bb]], axis=1),
                                         jnp.concatenate([x_[:, sl(h)] for x_ in kcat[bb]], axis=1),
                                         NT_DIMS, preferred_element_type=F32)
        upd = lax.dot_general(vbs[bb][:, sl(h)], kst[bb][:, sl(h)], TN_DIMS, preferred_element_type=F32)
        st_ref[bb, h] = dec[bb][:, sl(h)] * st + upd
    if not need_o:
        return None
    outs = [[] for _ in seqs]
    for bb, h in heads:
        a_ = jnp.where(mask, att[bb, h], 0.0).astype(BF16)
        outs[bb].append(inter[bb, h] + jnp.dot(a_, vbs[bb][:, sl(h)], preferred_element_type=F32))
    return [jnp.concatenate(o_, axis=1) for o_ in outs]


def _hgrn_kernel(*refs, reverse, mode, nchunk):
    if mode == "state":
        f_ref, v_ref, s0_ref, lb_ref, tri_ref, sfin_ref, st_ref = refs
        q_ref = None
    elif mode == "fwd":
        f_ref, v_ref, q_ref, s0_ref, lb_ref, tri_ref, o_ref, st_ref = refs
    else:
        (f_ref, v_ref, q_ref, og_ref, op_ref, s0_ref, lb_ref, tri_ref, nw_ref, wa_ref,
         y_ref, st_ref, gs_ref) = refs
    j = pl.program_id(1)
    nb = f_ref.shape[0]

    @pl.when(j == 0)
    def _():
        st_ref[...] = s0_ref[...]

    lb = lb_ref[...]
    tri = tri_ref[...]

    def body(ci, carry):
        cc = nchunk - 1 - ci if reverse else ci
        r0 = pl.multiple_of(cc * CHUNK, CHUNK)
        rs = pl.ds(r0, CHUNK)
        seqs = []
        for bb in range(nb):
            q = None if mode == "state" else _silu(q_ref[bb, rs, :].astype(F32))
            seqs.append((f_ref[bb, rs, :].astype(F32), v_ref[bb, rs, :].astype(F32), q))
        outs = _hgrn_chunks(seqs, lb, tri, st_ref, reverse, mode != "state")
        if mode == "state":
            return carry
        for bb in range(nb):
            o = outs[bb]
            if mode == "fwd":
                o_ref[bb, rs, :] = o.astype(BF16)
                continue
            o = o + op_ref[bb, rs, :].astype(F32)
            nw = nw_ref[...]
            parts = []
            for h in range(HG_HEADS):
                oh = o[:, h * HG_DK:(h + 1) * HG_DK]
                ms = jnp.mean(oh * oh, -1, keepdims=True)
                parts.append(oh * lax.rsqrt(ms + LN_EPS) * nw)
            on = jnp.concatenate(parts, axis=1)
            gs_ref[bb, rs, :] = (on * _silu(og_ref[bb, rs, :].astype(F32))).astype(BF16)
        return carry

    lax.fori_loop(0, nchunk, body, 0)

    if mode == "state":
        @pl.when(j == pl.num_programs(1) - 1)
        def _():
            sfin_ref[...] = st_ref[...]
    elif mode == "rev_out":
        for bb in range(nb):
            y_ref[bb] = jnp.dot(gs_ref[bb], wa_ref[...], preferred_element_type=F32)


def _tri(n, reverse, block=None):
    block = n if block is None else block
    i = np.arange(n)
    same = (i[:, None] // block) == (i[None, :] // block)
    m = (i[None, :] >= i[:, None]) if reverse else (i[None, :] <= i[:, None])
    return jnp.asarray((m & same).astype(np.float32))


def _hgrn(proj, s0, lb, reverse, mode, tt, o_prev=None, norm_w=None, w_a=None):
    b, t, _ = proj.shape
    nt = t // tt
    hw = HG_WIDTH
    nb = HG_NB if b % HG_NB == 0 else 1
    grid = (b // nb, nt)

    def tmap(col):
        if reverse:
            return lambda bi, j: (bi, nt - 1 - j, col)
        return lambda bi, j: (bi, j, col)

    pspec = lambda col: pl.BlockSpec((nb, tt, hw), tmap(col))
    sspec = pl.BlockSpec((nb, HG_HEADS, HG_DK, HG_DK), lambda bi, j: (bi, 0, 0, 0))
    cspec = lambda shape: pl.BlockSpec(shape, lambda bi, j: (0,) * len(shape))
    fcol = PB_FB if reverse else PB_FF
    tri = _tri(CHUNK, reverse)
    kern = functools.partial(_hgrn_kernel, reverse=reverse, mode=mode, nchunk=tt // CHUNK)
    st_scr = pltpu.VMEM((nb, HG_HEADS, HG_DK, HG_DK), F32)
    if mode == "state":
        return pl.pallas_call(
            kern, grid=grid,
            in_specs=[pspec(fcol), pspec(PB_IV), sspec, cspec((1, hw)), cspec((CHUNK, CHUNK))],
            out_specs=sspec,
            out_shape=jax.ShapeDtypeStruct(s0.shape, F32),
            scratch_shapes=[st_scr],
            compiler_params=_cparams(("arbitrary", "arbitrary")),
            name="hgrn_state",
        )(proj, proj, s0, lb, tri)
    if mode == "fwd":
        return pl.pallas_call(
            kern, grid=grid,
            in_specs=[pspec(fcol), pspec(PB_IV), pspec(PB_Q), sspec, cspec((1, hw)), cspec((CHUNK, CHUNK))],
            out_specs=pl.BlockSpec((nb, tt, hw), tmap(0)),
            out_shape=jax.ShapeDtypeStruct((b, t, hw), BF16),
            scratch_shapes=[st_scr],
            compiler_params=_cparams(("arbitrary", "arbitrary")),
            name="hgrn_fwd",
        )(proj, proj, proj, s0, lb, tri)
    return pl.pallas_call(
        kern, grid=grid,
        in_specs=[pspec(fcol), pspec(PB_IV), pspec(PB_Q), pspec(PB_OG),
                  pl.BlockSpec((nb, tt, hw), tmap(0)), sspec, cspec((1, hw)), cspec((CHUNK, CHUNK)),
                  cspec((1, HG_DK)), cspec((hw, D_MODEL))],
        out_specs=pl.BlockSpec((nb, tt, D_MODEL), tmap(0)),
        out_shape=jax.ShapeDtypeStruct((b, t, D_MODEL), F32),
        scratch_shapes=[st_scr, pltpu.VMEM((nb, tt, hw), BF16)],
        compiler_params=_cparams(("arbitrary", "arbitrary")),
        name="hgrn_rev",
    )(proj, proj, proj, proj, o_prev, s0, lb, tri, norm_w, w_a)


def _conv_silu(xp, w, b):
    r = xp.shape[0]
    rio = lax.broadcasted_iota(I32, (r, 1), 0)
    acc = xp * w[MA_CONV // 2:MA_CONV // 2 + 1] + b
    for kk in range(MA_CONV):
        d = kk - MA_CONV // 2
        if d == 0:
            continue
        sh = pltpu.roll(xp, (-d) % r, axis=0)
        valid = jnp.logical_and(rio + d >= 0, rio + d < r)
        acc = acc + jnp.where(valid, sh, 0.0) * w[kk:kk + 1]
    return _silu(acc)


def _expand(x, e):
    hi = x.astype(BF16)
    lo = (x - hi.astype(F32)).astype(BF16)
    return (jnp.dot(hi, e, preferred_element_type=F32) + jnp.dot(lo, e, preferred_element_type=F32))


def _ssd_kernel(*refs, reverse, mode, nrows):
    if mode == "state":
        (x0_ref, x1_ref, b_ref, c_ref, dt_ref, h0_ref, cw_ref, cb_ref, dtb_ref, a_ref, e_ref,
         tri_ref, hfin_ref, hst_ref) = refs
    elif mode == "fwd":
        (x0_ref, x1_ref, b_ref, c_ref, dt_ref, h0_ref, cw_ref, cb_ref, dtb_ref, a_ref, e_ref,
         tri_ref, y_ref, xc_ref, hst_ref) = refs
    else:
        (x0_ref, x1_ref, b_ref, c_ref, dt_ref, h0_ref, cw_ref, cb_ref, dtb_ref, a_ref, e_ref,
         tri_ref, z0_ref, z1_ref, yf_ref, dsk_ref, nw_ref, wb_ref, y_ref, hst_ref, ys_ref) = refs
    j = pl.program_id(1)

    @pl.when(j == 0)
    def _():
        hst_ref[...] = h0_ref[0]

    cw = cw_ref[...]
    cb = cb_ref[...]
    need_y = mode != "state"
    pieces = []
    for pi, pref in enumerate((x0_ref, x1_ref, b_ref, c_ref)):
        if pi == 3 and not need_y:
            pieces.append(None)
            continue
        cs = slice(pi * 1024, (pi + 1) * 1024)
        if mode == "rev_out":
            pieces.append(pref[0, 0].astype(F32))
            continue
        pieces.append(_conv_silu(pref[0, 0].astype(F32), cw[:, cs], cb[:, cs]))
        if mode == "fwd":
            xc_ref[0, 0, :, cs] = pieces[-1].astype(BF16)
    xh = jnp.concatenate(pieces[:2], axis=1)
    bm = pieces[2].astype(BF16)
    cm = pieces[3].astype(BF16) if need_y else None

    x = dt_ref[0, 0] + dtb_ref[...]
    dt = jnp.maximum(x, 0.0) + jnp.log1p(jnp.exp(-jnp.abs(x)))
    la = dt * a_ref[...]
    e_mat = e_ref[...]
    tri = tri_ref[...]
    lane0 = MA_HEADS if reverse else 0
    rows = lax.broadcasted_iota(I32, (CHUNK, CHUNK), 0)
    cols = lax.broadcasted_iota(I32, (CHUNK, CHUNK), 1)
    mask = (cols >= rows) if reverse else (cols <= rows)
    lane128 = lax.broadcasted_iota(I32, (CHUNK, 128), 1)

    npair = nrows // 128
    for pbi in range(npair):
        pb = npair - 1 - pbi if reverse else pbi
        prow = slice(pb * 128, (pb + 1) * 128)
        lcum2 = _dot_hilo(tri.astype(BF16), la[prow])
        lcum_t = lcum2.T
        dt_t = dt[prow].T
        e = 0 if reverse else CHUNK - 1

        def cb_product(cc, g):
            crow = slice(pb * 128 + cc * CHUNK, pb * 128 + (cc + 1) * CHUNK)
            gs = slice(g * MA_STATE, (g + 1) * MA_STATE)
            return lax.dot_general(cm[crow][:, gs], bm[crow][:, gs], NT_DIMS, preferred_element_type=F32)

        def chunk_pre(cc, with_cb):
            crow = slice(pb * 128 + cc * CHUNK, pb * 128 + (cc + 1) * CHUNK)
            lc = lcum2[cc * CHUNK:(cc + 1) * CHUNK]
            llast = lc[e:e + 1]
            p_ = {"w_out": _expand(dt[crow] * jnp.exp(llast - lc), e_mat),
                  "expl": _expand(jnp.broadcast_to(jnp.exp(llast), (8, 128)), e_mat)[0:1]}
            if need_y:
                p_["dec_in"] = _expand(jnp.exp(lc), e_mat)
                if with_cb:
                    p_["cbg"] = [cb_product(cc, g) for g in range(MA_GROUPS)]
            return p_

        hoist = mode == "rev_out"
        pre = {cc: chunk_pre(cc, True) for cc in range(2)} if hoist else {}
        for ci in range(2):
            cc = 1 - ci if reverse else ci
            if not hoist:
                pre[cc] = chunk_pre(cc, False)
            crow = slice(pb * 128 + cc * CHUNK, pb * 128 + (cc + 1) * CHUNK)
            ccol = slice(cc * CHUNK, (cc + 1) * CHUNK)
            lc = lcum2[ccol]
            xh_c = xh[crow]
            xb_c = xh_c.astype(BF16)
            xd = (xh_c * pre[cc]["w_out"]).astype(BF16)
            expl = pre[cc]["expl"]
            bc = bm[crow]
            if need_y:
                dec_in = pre[cc]["dec_in"]
                cc_ = cm[crow]
            ys = []
            for g in range(MA_GROUPS):
                gs = slice(g * MA_STATE, (g + 1) * MA_STATE)
                gw = slice(g * MA_GW, (g + 1) * MA_GW)
                hg = hst_ref[g]
                if need_y:
                    cbg = pre[cc]["cbg"][g] if hoist else cb_product(cc, g)
                    pair_out = []
                    for jp in range(2):
                        ms = []
                        for jj in range(2):
                            ln = lane0 + 4 * g + 2 * jp + jj
                            diff = lc[:, ln:ln + 1] - lcum_t[ln:ln + 1, ccol]
                            lmat = jnp.exp(jnp.where(mask, diff, NEG))
                            ms.append(cbg * lmat * dt_t[ln:ln + 1, ccol])
                        lhs = jnp.concatenate(ms, axis=0).astype(BF16)
                        xs = xb_c[:, g * MA_GW + jp * 128:g * MA_GW + (jp + 1) * 128]
                        res = jnp.dot(lhs, xs, preferred_element_type=F32)
                        pair_out.append(jnp.where(lane128 < MA_HEAD_DIM, res[:CHUNK], res[CHUNK:]))
                    y_intra = jnp.concatenate(pair_out, axis=1)
                    y_inter = jnp.dot(cc_[:, gs], hg.astype(BF16), preferred_element_type=F32)
                    ys.append(y_intra + y_inter * dec_in[:, gw])
                upd = lax.dot_general(bc[:, gs], xd[:, gw], TN_DIMS, preferred_element_type=F32)
                hst_ref[g] = expl[:, gw] * hg + upd
            if not need_y:
                continue
            y = jnp.concatenate(ys, axis=1)
            if mode == "fwd":
                y_ref[0, 0, crow, :] = y.astype(BF16)
            else:
                ytot = yf_ref[0, 0, crow, :].astype(F32) + y + xh_c * dsk_ref[...]
                zc = jnp.concatenate([z0_ref[0, 0, crow, :], z1_ref[0, 0, crow, :]], axis=1).astype(F32)
                yz = ytot * _silu(zc)
                nw = nw_ref[...]
                parts = []
                for g in range(MA_GROUPS):
                    gw = slice(g * MA_GW, (g + 1) * MA_GW)
                    yg = yz[:, gw]
                    ms_ = jnp.mean(yg * yg, -1, keepdims=True)
                    parts.append(yg * lax.rsqrt(ms_ + LN_EPS) * nw[:, gw])
                ys_ref[crow, :] = jnp.concatenate(parts, axis=1).astype(BF16)

    if mode == "state":
        @pl.when(j == pl.num_programs(1) - 1)
        def _():
            hfin_ref[0] = hst_ref[...]
    elif mode == "rev_out":
        y_ref[0, 0] = jnp.dot(ys_ref[...], wb_ref[...], preferred_element_type=F32)


def _ssd(pcm, dtraw, h0, consts, reverse, mode, y_f=None, xconv=None, extra=None):
    b, ncols, nrows, _ = pcm.shape
    conv_w, conv_b, dt_bias, a128, e_mat = consts

    def wmap(col):
        if reverse:
            return lambda bi, j: (bi, ncols - 1 - j, 0, col)
        return lambda bi, j: (bi, j, 0, col)

    pspec = lambda col: pl.BlockSpec((1, 1, nrows, 1024), wmap(col))
    dspec = pl.BlockSpec((1, 1, nrows, DT_W), wmap(0))
    hspec = pl.BlockSpec((1, MA_GROUPS, MA_STATE, MA_GW), lambda bi, j: (bi, 0, 0, 0))
    cspec = lambda shape: pl.BlockSpec(shape, lambda bi, j: (0,) * len(shape))
    tri = _tri(128, reverse, block=CHUNK)
    kern = functools.partial(_ssd_kernel, reverse=reverse, mode=mode, nrows=nrows)
    base_specs = [pspec(PC_X0), pspec(PC_X1), pspec(PC_B), pspec(PC_C), dspec, hspec,
                  cspec((MA_CONV, 4096)), cspec((1, 4096)), cspec((1, DT_W)), cspec((1, DT_W)),
                  cspec((DT_W, MA_INNER)), cspec((128, 128))]
    xsrc = pcm if xconv is None else xconv
    base_args = [xsrc, xsrc, xsrc, xsrc, dtraw, h0, conv_w, conv_b, dt_bias, a128, e_mat, tri]
    h_scr = pltpu.VMEM((MA_GROUPS, MA_STATE, MA_GW), F32)
    sem = _cparams(("arbitrary", "arbitrary"))
    if mode == "state":
        return pl.pallas_call(
            kern, grid=(b, ncols), in_specs=base_specs, out_specs=hspec,
            out_shape=jax.ShapeDtypeStruct(h0.shape, F32), scratch_shapes=[h_scr],
            compiler_params=sem, name="ssd_state")(*base_args)
    if mode == "fwd":
        nconv = MA_INNER + 2 * MA_GROUPS * MA_STATE
        return pl.pallas_call(
            kern, grid=(b, ncols), in_specs=base_specs,
            out_specs=[pl.BlockSpec((1, 1, nrows, MA_INNER), wmap(0)),
                       pl.BlockSpec((1, 1, nrows, nconv), wmap(0))],
            out_shape=[jax.ShapeDtypeStruct((b, ncols, nrows, MA_INNER), BF16),
                       jax.ShapeDtypeStruct((b, ncols, nrows, nconv), BF16)],
            scratch_shapes=[h_scr], compiler_params=sem, name="ssd_fwd")(*base_args)
    d_skip, norm_w, w_b = extra
    return pl.pallas_call(
        kern, grid=(b, ncols),
        in_specs=base_specs + [pspec(PC_Z0), pspec(PC_Z1),
                               pl.BlockSpec((1, 1, nrows, MA_INNER), wmap(0)),
                               cspec((1, MA_INNER)), cspec((1, MA_INNER)), cspec((MA_INNER, D_MODEL))],
        out_specs=pl.BlockSpec((1, 1, nrows, D_MODEL), wmap(0)),
        out_shape=jax.ShapeDtypeStruct((b, ncols, nrows, D_MODEL), F32),
        scratch_shapes=[h_scr, pltpu.VMEM((nrows, MA_INNER), BF16)],
        compiler_params=sem, name="ssd_rev")(*base_args, pcm, pcm, y_f, d_skip, norm_w, w_b)


def _merge_kernel(x_ref, ya_ref, yb_ref, perm_ref, ga_ref, gb_ref, g1_ref, wo_ref, lg_ref, lb_ref, o_ref):
    tm, d = x_ref.shape
    yb = yb_ref[...].reshape(tm, d)
    hi = yb.astype(BF16)
    lo = (yb - hi.astype(F32)).astype(BF16)
    perm = perm_ref[...]
    yb = (jnp.dot(perm, hi, preferred_element_type=F32) + jnp.dot(perm, lo, preferred_element_type=F32))
    m = (_sigmoid(ga_ref[...].astype(F32)) * ya_ref[...] + _sigmoid(gb_ref[...].astype(F32)) * yb)
    out = jnp.dot(m.astype(BF16), wo_ref[...], preferred_element_type=F32)
    z = DN_ALPHA * x_ref[...] + g1_ref[0] * out
    o_ref[...] = _layer_norm(z) * lg_ref[...] + lb_ref[...]


def _merge(x2, y_a, y_b_cm, proj_rm, g1, w_out, ln_g, ln_b, tokens_per_batch):
    m, d = x2.shape
    ncols = y_b_cm.shape[1]
    tm = COL_TILE * ncols
    tpb = tokens_per_batch // tm
    tspec = pl.BlockSpec((tm, d), lambda i: (i, 0))
    cspec = lambda shape: pl.BlockSpec(shape, lambda i: (0,) * len(shape))
    perm = jnp.asarray(_perm_matrix(ncols, COL_TILE)).astype(BF16)
    return pl.pallas_call(
        _merge_kernel, grid=(m // tm,),
        in_specs=[tspec, tspec,
                  pl.BlockSpec((1, ncols, COL_TILE, d), lambda i: (i // tpb, 0, i % tpb, 0)),
                  cspec((tm, tm)),
                  pl.BlockSpec((tm, d), lambda i: (i, PB_GA)),
                  pl.BlockSpec((tm, d), lambda i: (i, PB_GB)),
                  pl.BlockSpec((1, 1, d), lambda i: (i // tpb, 0, 0)),
                  cspec((d, d)), cspec((1, d)), cspec((1, d))],
        out_specs=tspec,
        out_shape=jax.ShapeDtypeStruct((m, d), F32),
        compiler_params=_cparams(("arbitrary",)),
        name="merge",
    )(x2, y_a, y_b_cm, perm, proj_rm, proj_rm, g1, w_out, ln_g, ln_b)


KEY_BIAS = 0x40000000
KEY_MIN = 1.1754944e-38
LOW_MASK = PEER_NKEYS - 1


def _sort_pairs(n):
    def merge(lo, hi, r):
        step = r * 2
        if step < hi - lo:
            yield from merge(lo, hi, step)
            yield from merge(lo + r, hi, step)
            for i in range(lo + r, hi - r, step):
                yield (i, i + r)
        else:
            yield (lo, lo + r)

    def sort(lo, hi):
        if hi - lo >= 1:
            mid = lo + (hi - lo) // 2
            yield from sort(lo, mid)
            yield from sort(mid + 1, hi)
            yield from merge(lo, hi, 1)

    return list(sort(0, n - 1))


def _merge_pairs(n):
    out, j = [], n // 2
    while j:
        out += [(i, i | j) for i in range(n) if not i & j]
        j //= 2
    return out


def _cex(v, pairs):
    v = list(v)
    for i, j in pairs:
        v[i], v[j] = jnp.maximum(v[i], v[j]), jnp.minimum(v[i], v[j])
    return v


def _cex_p(v, c, pairs):
    v, c = list(v), list(c)
    for i, j in pairs:
        m = v[i] >= v[j]
        v[i], v[j], c[i], c[j] = (jnp.maximum(v[i], v[j]), jnp.minimum(v[i], v[j]),
                                  jnp.where(m, c[i], c[j]), jnp.where(m, c[j], c[i]))
    return v, c


def _to_key(s, low):
    b = pltpu.bitcast(s, I32)
    k = jnp.where(b < 0, jnp.bitwise_xor(b, 0x7FFFFFFF), b)
    k = jnp.right_shift(k, 1) + KEY_BIAS
    k = jnp.bitwise_or(jnp.bitwise_and(k, ~LOW_MASK), low)
    return pltpu.bitcast(k, F32)


def _from_key(kf):
    k = pltpu.bitcast(kf, I32)
    low = jnp.bitwise_and(k, LOW_MASK)
    k = jnp.left_shift(jnp.bitwise_and(k, ~LOW_MASK) - KEY_BIAS, 1)
    b = jnp.where(k < 0, jnp.bitwise_xor(k, 0x7FFFFFFF), k)
    return pltpu.bitcast(b, F32), low


def _top16_keys(st, sub):
    v = []
    for r in range(PEER_NKEYS // 8):
        v.append(_to_key(st[8 * r:8 * r + 8], (LOW_MASK - 8 * r) - sub))
    v = _cex(v, _sort_pairs(16))
    for sh in (4, 2, 1):
        part = [pltpu.roll(x, sh, axis=0) for x in v]
        v = _cex([jnp.maximum(v[r], part[15 - r]) for r in range(16)], _merge_pairs(16))
    return v


def _natural(lst, half, sub):
    x = lst[8 * half]
    for s_ in range(1, 8):
        x = jnp.where(sub == s_, lst[8 * half + s_], x)
    return x


def _slot_table():
    slots = {}
    for s_ in range(8):
        slots[(0, s_)] = (0, s_)
        slots[(1, s_)] = (0, 8 + s_)
        slots[(3, s_)] = (8 + s_, 0)
    for s_ in range(1, 8):
        slots[(2, s_)] = (s_, 0)
        slots[(4, s_)] = (1, s_)
    for s_ in range(2, 8):
        slots[(5, s_)] = (s_, 1)
    for s_ in (2, 3, 4):
        slots[(6, s_)] = (2, s_)
    for s_ in (2, 3):
        slots[(7, s_)] = (3, s_)
    slots[(7, 4)] = (4, 2)
    want = sorted((j, l) for j in range(PEER_TOPK) for l in range(PEER_TOPK) if (j + 1) * (l + 1) <= PEER_TOPK)
    assert sorted(slots.values()) == want
    low = -np.ones((8, 8), np.int32)
    for (q, s_), jl in slots.items():
        low[q, s_] = LOW_MASK - want.index(jl)
    return jnp.asarray(np.repeat(low.reshape(64, 1), 128, axis=1))


def _retrieve_head(q_ref, sk_ref, slot_low, sub, t0, h):
    vals, idxs = [], []
    for i in range(2):
        cidx = 2 * h + i
        qc = q_ref[pl.ds(t0, 128), cidx * PEER_HALF:(cidx + 1) * PEER_HALF]
        st = lax.dot_general(sk_ref[cidx], qc, NT_DIMS, preferred_element_type=F32)
        dec = [_from_key(k) for k in _top16_keys(st, sub)]
        vals.append([d[0] for d in dec])
        idxs.append([LOW_MASK - d[1] for d in dec])
    v1, v2 = vals
    i1, i2 = idxs
    s1n = [_natural(v1, hf, sub) for hf in range(2)]
    s2n = [_natural(v2, hf, sub) for hf in range(2)]
    i1n = [_natural(i1, hf, sub) * PEER_NKEYS for hf in range(2)]
    i2n = [_natural(i2, hf, sub) for hf in range(2)]
    row4 = sub == 4
    cand = [(v1[0] + s2n[0], i1[0] * PEER_NKEYS + i2n[0]),
            (v1[0] + s2n[1], i1[0] * PEER_NKEYS + i2n[1]),
            (s1n[0] + v2[0], i1n[0] + i2[0]),
            (s1n[1] + v2[0], i1n[1] + i2[0]),
            (v1[1] + s2n[0], i1[1] * PEER_NKEYS + i2n[0]),
            (s1n[0] + v2[1], i1n[0] + i2[1]),
            (v1[2] + s2n[0], i1[2] * PEER_NKEYS + i2n[0]),
            (jnp.where(row4, s1n[0] + v2[2], v1[3] + s2n[0]),
             jnp.where(row4, i1n[0] + i2[2], i1[3] * PEER_NKEYS + i2n[0]))]
    keys, codes = [], []
    for qi, (cv, cc) in enumerate(cand):
        low = slot_low[8 * qi:8 * qi + 8]
        keys.append(jnp.where(low >= 0, _to_key(cv, low), KEY_MIN))
        codes.append(cc)
    keys, codes = _cex_p(keys, codes, _sort_pairs(8))
    pk = [pltpu.roll(x, 4, axis=0) for x in keys]
    pc = [pltpu.roll(x, 4, axis=0) for x in codes]
    keys, codes = _cex_p(keys + pk[::-1], codes + pc[::-1], _merge_pairs(16))
    for sh in (2, 1):
        pk = [pltpu.roll(x, sh, axis=0) for x in keys]
        pc = [pltpu.roll(x, sh, axis=0) for x in codes]
        mk, mc = [], []
        for r in range(16):
            m = keys[r] >= pk[15 - r]
            mk.append(jnp.maximum(keys[r], pk[15 - r]))
            mc.append(jnp.where(m, codes[r], pc[15 - r]))
        keys, codes = _cex_p(mk, mc, _merge_pairs(16))
    tv = [_from_key(k)[0] for k in keys]
    ex = [jnp.exp(t - tv[0]) for t in tv]
    den = ex[0]
    for e_ in ex[1:]:
        den = den + e_
    inv = 1.0 / den
    return codes, [e_ * inv for e_ in ex]


def _peer_kernel(x_ref, sh_ref, sc_ref, g2_ref, wq_ref, sk_ref, slot_ref, u_ref, v_ref, lg_ref, lb_ref,
                 o_ref, xm_ref, q_ref, acc_ref, t_ref, tc_ref, tg_ref, ai_ref, bi_ref, gt_ref):
    s = pl.program_id(1)
    p = x_ref.shape[0]

    @pl.when(s == 0)
    def _():
        xm = (_layer_norm(x_ref[...]) * (1.0 + sc_ref[0]) + sh_ref[0]).astype(BF16)
        xm_ref[...] = xm
        acc_ref[...] = jnp.zeros_like(acc_ref)
        q_ref[...] = jnp.dot(xm, wq_ref[...], preferred_element_type=F32).astype(BF16)
        slot_low = slot_ref[...]
        sub = lax.broadcasted_iota(I32, (8, 128), 0)

        def tile(ti, carry):
            t0 = pl.multiple_of(ti * 128, 128)
            for h in range(PEER_HEADS):
                codes, gates = _retrieve_head(q_ref, sk_ref, slot_low, sub, t0, h)
                for r in range(PEER_TOPK):
                    row = h * PEER_TOPK + r
                    tc_ref[row:row + 1, :] = codes[r][0:1].astype(F32)
                    tg_ref[row:row + 1, :] = gates[r][0:1]
            ct = tc_ref[...].T.astype(I32)
            ai_ref[pl.ds(t0, 128), :] = jnp.right_shift(ct, 7).astype(F32)
            bi_ref[pl.ds(t0, 128), :] = jnp.bitwise_and(ct, LOW_MASK).astype(F32)
            gt_ref[pl.ds(t0, 128), :] = tg_ref[...].T
            return carry

        lax.fori_loop(0, p // 128, tile, 0)

        sub128 = lax.broadcasted_iota(I32, (PEER_NKEYS, 128), 0)
        amap = jnp.bitwise_or(jnp.left_shift(jnp.bitwise_and(sub128, 63), 1),
                              jnp.right_shift(sub128, 6)).astype(F32)
        bmap = sub128.astype(F32)

        def tok8(i, carry):
            t0 = pl.multiple_of(i * TOK_UNROLL, TOK_UNROLL)
            a8 = ai_ref[pl.ds(t0, TOK_UNROLL), :]
            b8 = bi_ref[pl.ds(t0, TOK_UNROLL), :]
            g8 = gt_ref[pl.ds(t0, TOK_UNROLL), :]
            for j in range(TOK_UNROLL):
                at = jnp.where(a8[j:j + 1] == amap, 1.0, 0.0).astype(BF16)
                rt = jnp.where(b8[j:j + 1] == bmap, g8[j:j + 1], 0.0).astype(BF16)
                w = lax.dot_general(at, rt, NT_DIMS, preferred_element_type=F32)
                lo = pltpu.bitcast(w[:64], I32) + 0x8000
                hi = pltpu.bitcast(w[64:], I32) + 0x8000
                lo = lax.shift_right_logical(lo, 16)
                hi = jnp.bitwise_and(hi, -65536)
                r0 = pl.multiple_of((t0 + j) * PEER_PITCH, 8)
                t_ref[pl.ds(r0, 64), :] = jnp.bitwise_or(lo, hi)
            return carry

        lax.fori_loop(0, p // TOK_UNROLL, tok8, 0)

    hval = lax.dot_general(xm_ref[...], u_ref[...], NT_DIMS, preferred_element_type=F32)
    act = 0.5 * hval * (1.0 + lax.erf(hval * 0.7071067811865476))
    wgts = []
    for g in range(PEER_G):
        wp = t_ref[pl.ds(s * PEER_G + g, p, stride=PEER_PITCH), :]
        wgts.append(pltpu.bitcast(jnp.left_shift(wp, 16), F32))
        wgts.append(pltpu.bitcast(jnp.bitwise_and(wp, -65536), F32))
    z = (jnp.concatenate(wgts, axis=1) * act).astype(BF16)
    acc_ref[...] += jnp.dot(z, v_ref[...], preferred_element_type=F32)

    @pl.when(s == pl.num_programs(1) - 1)
    def _():
        z = DN_ALPHA * x_ref[...] + g2_ref[0] * acc_ref[...]
        o_ref[...] = _layer_norm(z) * lg_ref[...] + lb_ref[...]


def _peer(x1, shift, scale, g2, wq, sk, u2, v2, ln_g, ln_b, tokens_per_batch):
    m, d = x1.shape
    p = PEER_P
    tpb = tokens_per_batch // p
    blk = PEER_G * 2 * PEER_NKEYS
    nsteps = u2.shape[0] // blk
    slot = _slot_table()
    cspec = lambda shape: pl.BlockSpec(shape, lambda i, s: (0,) * len(shape))
    bspec = pl.BlockSpec((1, 1, d), lambda i, s: (i // tpb, 0, 0))
    nent = PEER_HEADS * PEER_TOPK
    return pl.pallas_call(
        _peer_kernel, grid=(m // p, nsteps),
        in_specs=[pl.BlockSpec((p, d), lambda i, s: (i, 0)), bspec, bspec, bspec,
                  cspec(wq.shape), cspec(sk.shape), cspec(slot.shape),
                  pl.BlockSpec((blk, d), lambda i, s: (s, 0)),
                  pl.BlockSpec((blk, d), lambda i, s: (s, 0)),
                  cspec((1, d)), cspec((1, d))],
        out_specs=pl.BlockSpec((p, d), lambda i, s: (i, 0)),
        out_shape=jax.ShapeDtypeStruct((m, d), F32),
        scratch_shapes=[pltpu.VMEM((p, d), BF16), pltpu.VMEM((p, 2 * PEER_HEADS * PEER_HALF), BF16),
                        pltpu.VMEM((p, d), F32), pltpu.VMEM((p * PEER_PITCH, 128), I32),
                        pltpu.VMEM((nent, 128), F32), pltpu.VMEM((nent, 128), F32),
                        pltpu.VMEM((p, nent), F32), pltpu.VMEM((p, nent), F32), pltpu.VMEM((p, nent), F32)],
        compiler_params=_cparams(("arbitrary", "arbitrary")),
        name="peer",
    )(x1, shift, scale, g2, wq, sk, slot, u2, v2, ln_g, ln_b)


def _layer(x, c, ctx, c_ctx, lw, lb_l):
    b, t, d = x.shape
    tc = ctx.shape[1]

    npad = -(-(b + 1) // 8) * 8
    c_all = jnp.zeros((npad, d), F32).at[:b].set(c).at[b].set(c_ctx)
    mod = _ada(c_all, lw["w_ada"], lw["b_ada"])
    sh1, sc1, g1, sh2, sc2, g2 = [mod[:b, i * d:(i + 1) * d].reshape(b, 1, d) for i in range(6)]
    csh1 = jnp.broadcast_to(mod[b, 0:d], (b, 1, d))
    csc1 = jnp.broadcast_to(mod[b, d:2 * d], (b, 1, d))

    w_in = lw["w_in"]
    off = np.cumsum([0, HG_WIDTH, HG_WIDTH, HG_WIDTH, MA_INNER + MA_GROUPS * MA_STATE, MA_HEADS, MA_HEADS,
                     HG_WIDTH, HG_WIDTH, MA_GROUPS * MA_STATE, MA_INNER, D_MODEL, D_MODEL])
    o_ff, o_fb, o_iv, o_xb, o_dtf, o_dtb, o_q, o_og, o_cm, o_z, o_ga, o_gb = off[:12]
    col = lambda o, n: w_in[:, o:o + n]
    w_rm = jnp.concatenate([col(o_ff, 1024), col(o_fb, 1024), col(o_iv, 1024), col(o_q, 1024),
                            col(o_og, 1024), col(o_ga, 1024), col(o_gb, 1024)], axis=1).astype(BF16)
    w_cm = jnp.concatenate([col(o_xb, 3072), col(o_cm, 1024), col(o_z, 2048)], axis=1).astype(BF16)
    w_dt = jnp.concatenate([col(o_dtf, 2 * MA_HEADS), jnp.zeros((d, DT_W - 2 * MA_HEADS), F32)],
                           axis=1).astype(BF16)
    lb_f = lb_l[0].reshape(1, HG_WIDTH)
    lb_b = lb_l[1].reshape(1, HG_WIDTH)
    zpad = jnp.zeros((DT_W - 2 * MA_HEADS,), F32)
    dt_bias = jnp.concatenate([lw["ma_dt_bias"].reshape(-1), zpad]).reshape(1, DT_W)
    a128 = jnp.concatenate([-jnp.exp(lw["ma_a_log"].astype(F32)).reshape(-1), zpad]).reshape(1, DT_W)
    conv_w = lw["ma_conv_w"].T
    conv_b = lw["ma_conv_b"].reshape(1, -1)
    head_of = np.arange(MA_INNER) // MA_HEAD_DIM
    e_f = jnp.asarray((np.arange(DT_W)[:, None] == head_of[None, :]).astype(np.float32)).astype(BF16)
    e_b = jnp.asarray((np.arange(DT_W)[:, None] == (head_of + MA_HEADS)[None, :]).astype(np.float32)).astype(BF16)
    consts_f = (conv_w, conv_b, dt_bias, a128, e_f)
    consts_b = (conv_w, conv_b, dt_bias, a128, e_b)
    d_skip = jnp.repeat(lw["ma_d"], MA_HEAD_DIM).reshape(1, MA_INNER)
    ma_norm_w = lw["ma_norm_w"].reshape(1, MA_INNER)
    hg_norm_w = lw["hg_norm_w"].reshape(1, HG_DK)
    w_a = lw["w_branch_a"].astype(BF16)
    w_b = lw["w_branch_b"].astype(BF16)
    w_o = lw["w_out"].astype(BF16)

    projc = _inproj_rm(ctx.reshape(b * tc, d), csh1, csc1, w_rm, tc, tc).reshape(b, tc, NP_RM)
    pcmc, dtc = _inproj_cm(ctx, csh1, csc1, w_cm, w_dt, 1)
    zs = jnp.zeros((b, HG_HEADS, HG_DK, HG_DK), F32)
    zh = jnp.zeros((b, MA_GROUPS, MA_STATE, MA_GW), F32)
    s_hf = _hgrn(projc, zs, lb_f, False, "state", tc)
    s_hb = _hgrn(projc, zs, lb_b, True, "state", tc)
    h_f = _ssd(pcmc, dtc, zh, consts_f, False, "state")
    h_b = _ssd(pcmc, dtc, zh, consts_b, True, "state")

    x2 = x.reshape(b * t, d)
    proj = _inproj_rm(x2, sh1, sc1, w_rm, t, min(1024, t))
    pcm, dtr = _inproj_cm(x, sh1, sc1, w_cm, w_dt, GRID_W)
    proj3 = proj.reshape(b, t, NP_RM)
    tt = min(512, t)
    o_f = _hgrn(proj3, s_hf, lb_f, False, "fwd", tt)
    y_a = _hgrn(proj3, s_hb, lb_b, True, "rev_out", tt, o_prev=o_f, norm_w=hg_norm_w, w_a=w_a)
    y_f, xconv = _ssd(pcm, dtr, h_f, consts_f, False, "fwd")
    y_b = _ssd(pcm, dtr, h_b, consts_b, True, "rev_out", y_f=y_f, xconv=xconv,
               extra=(d_skip, ma_norm_w, w_b))
    x1 = _merge(x2, y_a.reshape(b * t, d), y_b, proj, g1, w_o,
                lw["ln1_g"].reshape(1, d), lw["ln1_b"].reshape(1, d), t)

    wq = lw["peer_wq"].astype(BF16)
    sk = lw["peer_subkeys"].reshape(PEER_HEADS * 2, PEER_NKEYS, PEER_HALF).astype(BF16)
    u2 = lw["peer_u"].astype(BF16)
    v2 = lw["peer_v"].astype(BF16)
    out = _peer(x1, sh2, sc2, g2, wq, sk, u2, v2, lw["ln2_g"].reshape(1, d), lw["ln2_b"].reshape(1, d), t)
    return out.reshape(b, t, d)


def kernel(x, c, ctx, c_ctx, w_ada, b_ada, w_in, hg_lb_logits, hg_norm_w, ma_conv_w, ma_conv_b, ma_dt_bias,
           ma_a_log, ma_d, ma_norm_w, w_branch_a, w_branch_b, w_out, ln1_g, ln1_b, peer_wq, peer_subkeys,
           peer_u, peer_v, ln2_g, ln2_b):
    depth = w_in.shape[0]
    assert depth == 1, "the scan states of a single (last) layer are implemented"
    lb_all = jnp.cumsum(jax.nn.softmax(hg_lb_logits.astype(F32), axis=1), axis=1)
    lw = {"w_ada": w_ada[0], "b_ada": b_ada[0], "w_in": w_in[0], "hg_norm_w": hg_norm_w[0],
          "ma_conv_w": ma_conv_w[0], "ma_conv_b": ma_conv_b[0], "ma_dt_bias": ma_dt_bias[0],
          "ma_a_log": ma_a_log[0], "ma_d": ma_d[0], "ma_norm_w": ma_norm_w[0],
          "w_branch_a": w_branch_a[0], "w_branch_b": w_branch_b[0], "w_out": w_out[0],
          "ln1_g": ln1_g[0], "ln1_b": ln1_b[0], "peer_wq": peer_wq[0], "peer_subkeys": peer_subkeys[0],
          "peer_u": peer_u[0], "peer_v": peer_v[0], "ln2_g": ln2_g[0], "ln2_b": ln2_b[0]}
    return _layer(x, c, ctx, c_ctx, lw, lb_all[:, 0])
```

```python
import functools

import numpy as np
import jax
import jax.numpy as jnp
from jax import lax
from jax.experimental import pallas as pl
from jax.experimental.pallas import tpu as pltpu

F32 = jnp.float32
BF16 = jnp.bfloat16
I32 = jnp.int32
HI = lax.Precision.HIGHEST

D_MODEL = 1024
GRID_W = 64
CHUNK = 64
SUB = 16
HG_NB = 2
HG_HEADS = 8
HG_DK = 128
HG_WIDTH = HG_HEADS * HG_DK
MA_INNER = 2 * D_MODEL
MA_HEAD_DIM = 64
MA_HEADS = MA_INNER // MA_HEAD_DIM
MA_GROUPS = 8
MA_STATE = 128
MA_GW = MA_INNER // MA_GROUPS
MA_CONV = 5
PEER_HEADS = 8
PEER_NKEYS = 128
PEER_TOPK = 16
PEER_HALF = 128
DN_ALPHA = 2.0 ** 0.25
LN_EPS = 1e-6
NEG = -3.0e38
EXP_CLAMP = 60.0

PB_FF, PB_FB, PB_IV, PB_Q, PB_OG, PB_GA, PB_GB = range(7)
PC_X0, PC_X1, PC_B, PC_C, PC_Z0, PC_Z1 = range(6)
NP_RM = 7 * 1024
NP_CM = 6 * 1024
DT_W = 128
COL_TILE = 8

PEER_P = 512
PEER_G = 4
PEER_PITCH = 72
TOK_UNROLL = 64
VMEM_LIMIT = 56 * 1024 * 1024

NT_DIMS = (((1,), (1,)), ((), ()))
TN_DIMS = (((0,), (0,)), ((), ()))


def _cparams(sem):
    return pltpu.CompilerParams(dimension_semantics=sem, vmem_limit_bytes=VMEM_LIMIT)


def _sigmoid(x):
    return 0.5 * jnp.tanh(0.5 * x) + 0.5


def _silu(x):
    return x * _sigmoid(x)


def _dot_hilo(m, x):
    hi = x.astype(BF16)
    lo = (x - hi.astype(F32)).astype(BF16)
    return jnp.dot(m, hi, preferred_element_type=F32) + jnp.dot(m, lo, preferred_element_type=F32)


def _layer_norm(x):
    mu = jnp.mean(x, -1, keepdims=True)
    xc = x - mu
    var = jnp.mean(xc * xc, -1, keepdims=True)
    return xc * lax.rsqrt(var + LN_EPS)


def _ada_kernel(c_ref, w_ref, b_ref, o_ref):
    c = c_ref[...]
    o_ref[...] = jnp.dot(_silu(c), w_ref[...], precision=HI, preferred_element_type=F32) + b_ref[...]


def _ada(c_all, w_ada, b_ada):
    n, d = c_all.shape
    e = w_ada.shape[1]
    tn = 1024
    return pl.pallas_call(
        _ada_kernel,
        grid=(e // tn,),
        in_specs=[pl.BlockSpec((n, d), lambda j: (0, 0)),
                  pl.BlockSpec((d, tn), lambda j: (0, j)),
                  pl.BlockSpec((1, tn), lambda j: (0, j))],
        out_specs=pl.BlockSpec((n, tn), lambda j: (0, j)),
        out_shape=jax.ShapeDtypeStruct((n, e), F32),
        compiler_params=_cparams(("arbitrary",)),
        name="ada",
    )(c_all, w_ada, b_ada.reshape(1, e))


def _inproj_rm_kernel(x_ref, sh_ref, sc_ref, w_ref, o_ref, u_ref):
    @pl.when(pl.program_id(1) == 0)
    def _():
        u = _layer_norm(x_ref[...]) * (1.0 + sc_ref[0]) + sh_ref[0]
        u_ref[...] = u.astype(BF16)

    o_ref[...] = jnp.dot(u_ref[...], w_ref[...], preferred_element_type=F32).astype(BF16)


def _inproj_rm(x2, shift, scale, w_rm, tokens_per_batch, tm):
    m, d = x2.shape
    tpb = tokens_per_batch // tm
    tn = NP_RM // 2
    return pl.pallas_call(
        _inproj_rm_kernel,
        grid=(m // tm, NP_RM // tn),
        in_specs=[pl.BlockSpec((tm, d), lambda i, j: (i, 0)),
                  pl.BlockSpec((1, 1, d), lambda i, j: (i // tpb, 0, 0)),
                  pl.BlockSpec((1, 1, d), lambda i, j: (i // tpb, 0, 0)),
                  pl.BlockSpec((d, tn), lambda i, j: (0, j))],
        out_specs=pl.BlockSpec((tm, tn), lambda i, j: (i, j)),
        out_shape=jax.ShapeDtypeStruct((m, NP_RM), BF16),
        scratch_shapes=[pltpu.VMEM((tm, d), BF16)],
        compiler_params=_cparams(("arbitrary", "arbitrary")),
        name="inproj_rm",
    )(x2, shift, scale, w_rm)


def _inproj_cm_kernel(*refs, permute):
    if permute:
        x_ref, sh_ref, sc_ref, perm_ref, w_ref, wdt_ref, o_ref, dt_ref, u_ref = refs
    else:
        x_ref, sh_ref, sc_ref, w_ref, wdt_ref, o_ref, dt_ref, u_ref = refs
    tm, d = u_ref.shape

    @pl.when(pl.program_id(2) == 0)
    def _():
        x = x_ref[...].reshape(tm, d)
        ub = (_layer_norm(x) * (1.0 + sc_ref[0]) + sh_ref[0]).astype(BF16)
        if permute:
            ub = jnp.dot(perm_ref[...], ub, preferred_element_type=F32).astype(BF16)
        u_ref[...] = ub
        dt_ref[...] = jnp.dot(ub, wdt_ref[...], preferred_element_type=F32).reshape(dt_ref.shape)

    o_ref[...] = jnp.dot(u_ref[...], w_ref[...], preferred_element_type=F32).astype(BF16).reshape(o_ref.shape)


def _perm_matrix(nrows, ncols):
    n = nrows * ncols
    p = np.zeros((n, n), np.float32)
    r, w = np.meshgrid(np.arange(nrows), np.arange(ncols), indexing="ij")
    p[(w * nrows + r).ravel(), (r * ncols + w).ravel()] = 1.0
    return p


def _inproj_cm(x, shift, scale, w_cm, w_dt, ncols):
    b, t, d = x.shape
    rows = t // ncols
    tn = NP_CM // 2
    cspec = lambda shape: pl.BlockSpec(shape, lambda bi, wi, j: (0,) * len(shape))
    mspec = pl.BlockSpec((1, 1, d), lambda bi, wi, j: (bi, 0, 0))
    if ncols == 1:
        ct, tm = 1, rows
        xv = x
        xspec = pl.BlockSpec((1, rows, d), lambda bi, wi, j: (bi, 0, 0))
        extra, especs = [], []
    else:
        ct, tm = COL_TILE, rows * COL_TILE
        xv = x.reshape(b, rows, ncols, d)
        xspec = pl.BlockSpec((1, rows, ct, d), lambda bi, wi, j: (bi, 0, wi, 0))
        extra = [jnp.asarray(_perm_matrix(rows, ct)).astype(BF16)]
        especs = [cspec((tm, tm))]
    return pl.pallas_call(
        functools.partial(_inproj_cm_kernel, permute=ncols > 1),
        grid=(b, ncols // ct, NP_CM // tn),
        in_specs=[xspec, mspec, mspec] + especs +
                 [pl.BlockSpec((d, tn), lambda bi, wi, j: (0, j)), cspec((d, DT_W))],
        out_specs=[pl.BlockSpec((1, ct, rows, tn), lambda bi, wi, j: (bi, wi, 0, j)),
                   pl.BlockSpec((1, ct, rows, DT_W), lambda bi, wi, j: (bi, wi, 0, 0))],
        out_shape=[jax.ShapeDtypeStruct((b, ncols, rows, NP_CM), BF16),
                   jax.ShapeDtypeStruct((b, ncols, rows, DT_W), F32)],
        scratch_shapes=[pltpu.VMEM((tm, d), BF16)],
        compiler_params=_cparams(("arbitrary", "arbitrary", "arbitrary")),
        name="inproj_cm",
    )(xv, shift, scale, *extra, w_cm, w_dt)


def _hgrn_chunks(seqs, lb, tri, st_ref, reverse, need_o):
    c = seqs[0][0].shape[0]
    heads = [(bb, h) for bb in range(len(seqs)) for h in range(HG_HEADS)]
    sl = lambda h: slice(h * HG_DK, (h + 1) * HG_DK)
    rows = lax.broadcasted_iota(I32, (c, c), 0)
    cols = lax.broadcasted_iota(I32, (c, c), 1)
    mask = (cols >= rows) if reverse else (cols <= rows)
    tri_b = tri.astype(BF16)
    pre = []
    for fraw, v, q in seqs:
        fg = lb + (1.0 - lb) * _sigmoid(fraw)
        logf = jnp.log(fg)
        pre.append((1.0 - fg, logf, _dot_hilo(tri_b, logf), v.astype(BF16), q))
    e = 0 if reverse else c - 1
    kst, dec, qin, qcat, kcat, vbs = [], [], [], [], [], []
    for k, logf, cum, vb, q in pre:
        glast = cum[e:e + 1]
        kst.append((k * jnp.exp(glast - cum)).astype(BF16))
        dec.append(jnp.exp(glast))
        vbs.append(vb)
        if not need_o:
            continue
        qin.append((q * jnp.exp(cum)).astype(BF16))
        qhat, khat = [], []
        for i in range(c // SUB):
            ei = i * SUB + SUB - 1 if reverse else i * SUB
            ref = cum[ei:ei + 1] - logf[ei:ei + 1]
            blk = slice(i * SUB, (i + 1) * SUB)
            qh = (q[blk] * jnp.exp(cum[blk] - ref)).astype(BF16)
            pads = [jnp.zeros((SUB, qh.shape[1]), BF16)] * (c // SUB)
            pads[i] = qh
            qhat.append(jnp.concatenate(pads, axis=0))
            khat.append((k * jnp.exp(jnp.minimum(ref - cum, EXP_CLAMP))).astype(BF16))
        qcat.append(qhat)
        kcat.append(khat)
    att, inter = {}, {}
    for bb, h in heads:
        st = st_ref[bb, h]
        if need_o:
            inter[bb, h] = lax.dot_general(qin[bb][:, sl(h)], st.astype(BF16), NT_DIMS,
                                           preferred_element_type=F32)
            att[bb, h] = lax.dot_general(jnp.concatenate([x_[:, sl(h)] for x_ in qcat[bb]], axis=1),
                                         jnp.concatenate([x_[:, sl(h)] for x_ in kcat[bb]], axis=1),
                                         NT_DIMS, preferred_element_type=F32)
        upd = lax.dot_general(vbs[bb][:, sl(h)], kst[bb][:, sl(h)], TN_DIMS, preferred_element_type=F32)
        st_ref[bb, h] = dec[bb][:, sl(h)] * st + upd
    if not need_o:
        return None
    outs = [[] for _ in seqs]
    for bb, h in heads:
        a_ = jnp.where(mask, att[bb, h], 0.0).astype(BF16)
        outs[bb].append(inter[bb, h] + jnp.dot(a_, vbs[bb][:, sl(h)], preferred_element_type=F32))
    return [jnp.concatenate(o_, axis=1) for o_ in outs]


def _hgrn_kernel(*refs, reverse, mode, nchunk):
    if mode == "state":
        f_ref, v_ref, s0_ref, lb_ref, tri_ref, sfin_ref, st_ref = refs
        q_ref = None
    elif mode == "fwd":
        f_ref, v_ref, q_ref, s0_ref, lb_ref, tri_ref, o_ref, st_ref = refs
    else:
        (f_ref, v_ref, q_ref, og_ref, op_ref, s0_ref, lb_ref, tri_ref, nw_ref, wa_ref,
         y_ref, st_ref, gs_ref) = refs
    j = pl.program_id(1)
    nb = f_ref.shape[0]

    @pl.when(j == 0)
    def _():
        st_ref[...] = s0_ref[...]

    lb = lb_ref[...]
    tri = tri_ref[...]

    def body(ci, carry):
        cc = nchunk - 1 - ci if reverse else ci
        r0 = pl.multiple_of(cc * CHUNK, CHUNK)
        rs = pl.ds(r0, CHUNK)
        seqs = []
        for bb in range(nb):
            q = None if mode == "state" else _silu(q_ref[bb, rs, :].astype(F32))
            seqs.append((f_ref[bb, rs, :].astype(F32), v_ref[bb, rs, :].astype(F32), q))
        outs = _hgrn_chunks(seqs, lb, tri, st_ref, reverse, mode != "state")
        if mode == "state":
            return carry
        for bb in range(nb):
            o = outs[bb]
            if mode == "fwd":
                o_ref[bb, rs, :] = o.astype(BF16)
                continue
            o = o + op_ref[bb, rs, :].astype(F32)
            nw = nw_ref[...]
            parts = []
            for h in range(HG_HEADS):
                oh = o[:, h * HG_DK:(h + 1) * HG_DK]
                ms = jnp.mean(oh * oh, -1, keepdims=True)
                parts.append(oh * lax.rsqrt(ms + LN_EPS) * nw)
            on = jnp.concatenate(parts, axis=1)
            gs_ref[bb, rs, :] = (on * _silu(og_ref[bb, rs, :].astype(F32))).astype(BF16)
        return carry

    lax.fori_loop(0, nchunk, body, 0)

    if mode == "state":
        @pl.when(j == pl.num_programs(1) - 1)
        def _():
            sfin_ref[...] = st_ref[...]
    elif mode == "rev_out":
        for bb in range(nb):
            y_ref[bb] = jnp.dot(gs_ref[bb], wa_ref[...], preferred_element_type=F32)


def _tri(n, reverse, block=None):
    block = n if block is None else block
    i = np.arange(n)
    same = (i[:, None] // block) == (i[None, :] // block)
    m = (i[None, :] >= i[:, None]) if reverse else (i[None, :] <= i[:, None])
    return jnp.asarray((m & same).astype(np.float32))


def _hgrn(proj, s0, lb, reverse, mode, tt, o_prev=None, norm_w=None, w_a=None):
    b, t, _ = proj.shape
    nt = t // tt
    hw = HG_WIDTH
    nb = HG_NB if b % HG_NB == 0 else 1
    grid = (b // nb, nt)

    def tmap(col):
        if reverse:
            return lambda bi, j: (bi, nt - 1 - j, col)
        return lambda bi, j: (bi, j, col)

    pspec = lambda col: pl.BlockSpec((nb, tt, hw), tmap(col))
    sspec = pl.BlockSpec((nb, HG_HEADS, HG_DK, HG_DK), lambda bi, j: (bi, 0, 0, 0))
    cspec = lambda shape: pl.BlockSpec(shape, lambda bi, j: (0,) * len(shape))
    fcol = PB_FB if reverse else PB_FF
    tri = _tri(CHUNK, reverse)
    kern = functools.partial(_hgrn_kernel, reverse=reverse, mode=mode, nchunk=tt // CHUNK)
    st_scr = pltpu.VMEM((nb, HG_HEADS, HG_DK, HG_DK), F32)
    if mode == "state":
        return pl.pallas_call(
            kern, grid=grid,
            in_specs=[pspec(fcol), pspec(PB_IV), sspec, cspec((1, hw)), cspec((CHUNK, CHUNK))],
            out_specs=sspec,
            out_shape=jax.ShapeDtypeStruct(s0.shape, F32),
            scratch_shapes=[st_scr],
            compiler_params=_cparams(("arbitrary", "arbitrary")),
            name="hgrn_state",
        )(proj, proj, s0, lb, tri)
    if mode == "fwd":
        return pl.pallas_call(
            kern, grid=grid,
            in_specs=[pspec(fcol), pspec(PB_IV), pspec(PB_Q), sspec, cspec((1, hw)), cspec((CHUNK, CHUNK))],
            out_specs=pl.BlockSpec((nb, tt, hw), tmap(0)),
            out_shape=jax.ShapeDtypeStruct((b, t, hw), BF16),
            scratch_shapes=[st_scr],
            compiler_params=_cparams(("arbitrary", "arbitrary")),
            name="hgrn_fwd",
        )(proj, proj, proj, s0, lb, tri)
    return pl.pallas_call(
        kern, grid=grid,
        in_specs=[pspec(fcol), pspec(PB_IV), pspec(PB_Q), pspec(PB_OG),
                  pl.BlockSpec((nb, tt, hw), tmap(0)), sspec, cspec((1, hw)), cspec((CHUNK, CHUNK)),
                  cspec((1, HG_DK)), cspec((hw, D_MODEL))],
        out_specs=pl.BlockSpec((nb, tt, D_MODEL), tmap(0)),
        out_shape=jax.ShapeDtypeStruct((b, t, D_MODEL), F32),
        scratch_shapes=[st_scr, pltpu.VMEM((nb, tt, hw), BF16)],
        compiler_params=_cparams(("arbitrary", "arbitrary")),
        name="hgrn_rev",
    )(proj, proj, proj, proj, o_prev, s0, lb, tri, norm_w, w_a)


def _conv_silu(xp, w, b):
    r = xp.shape[0]
    rio = lax.broadcasted_iota(I32, (r, 1), 0)
    acc = xp * w[MA_CONV // 2:MA_CONV // 2 + 1] + b
    for kk in range(MA_CONV):
        d = kk - MA_CONV // 2
        if d == 0:
            continue
        sh = pltpu.roll(xp, (-d) % r, axis=0)
        valid = jnp.logical_and(rio + d >= 0, rio + d < r)
        acc = acc + jnp.where(valid, sh, 0.0) * w[kk:kk + 1]
    return _silu(acc)


def _expand(x, e):
    hi = x.astype(BF16)
    lo = (x - hi.astype(F32)).astype(BF16)
    return (jnp.dot(hi, e, preferred_element_type=F32) + jnp.dot(lo, e, preferred_element_type=F32))


def _ssd_kernel(*refs, reverse, mode, nrows):
    if mode == "state":
        (x0_ref, x1_ref, b_ref, c_ref, dt_ref, h0_ref, cw_ref, cb_ref, dtb_ref, a_ref, e_ref,
         tri_ref, hfin_ref, hst_ref) = refs
    elif mode == "fwd":
        (x0_ref, x1_ref, b_ref, c_ref, dt_ref, h0_ref, cw_ref, cb_ref, dtb_ref, a_ref, e_ref,
         tri_ref, y_ref, xc_ref, hst_ref) = refs
    else:
        (x0_ref, x1_ref, b_ref, c_ref, dt_ref, h0_ref, cw_ref, cb_ref, dtb_ref, a_ref, e_ref,
         tri_ref, z0_ref, z1_ref, yf_ref, dsk_ref, nw_ref, wb_ref, y_ref, hst_ref, ys_ref) = refs
    j = pl.program_id(1)

    @pl.when(j == 0)
    def _():
        hst_ref[...] = h0_ref[0]

    cw = cw_ref[...]
    cb = cb_ref[...]
    need_y = mode != "state"
    pieces = []
    for pi, pref in enumerate((x0_ref, x1_ref, b_ref, c_ref)):
        if pi == 3 and not need_y:
            pieces.append(None)
            continue
        cs = slice(pi * 1024, (pi + 1) * 1024)
        if mode == "rev_out":
            pieces.append(pref[0, 0].astype(F32))
            continue
        pieces.append(_conv_silu(pref[0, 0].astype(F32), cw[:, cs], cb[:, cs]))
        if mode == "fwd":
            xc_ref[0, 0, :, cs] = pieces[-1].astype(BF16)
    xh = jnp.concatenate(pieces[:2], axis=1)
    bm = pieces[2].astype(BF16)
    cm = pieces[3].astype(BF16) if need_y else None

    x = dt_ref[0, 0] + dtb_ref[...]
    dt = jnp.maximum(x, 0.0) + jnp.log1p(jnp.exp(-jnp.abs(x)))
    la = dt * a_ref[...]
    e_mat = e_ref[...]
    tri = tri_ref[...]
    lane0 = MA_HEADS if reverse else 0
    rows = lax.broadcasted_iota(I32, (CHUNK, CHUNK), 0)
    cols = lax.broadcasted_iota(I32, (CHUNK, CHUNK), 1)
    mask = (cols >= rows) if reverse else (cols <= rows)
    lane128 = lax.broadcasted_iota(I32, (CHUNK, 128), 1)

    npair = nrows // 128
    for pbi in range(npair):
        pb = npair - 1 - pbi if reverse else pbi
        prow = slice(pb * 128, (pb + 1) * 128)
        lcum2 = _dot_hilo(tri.astype(BF16), la[prow])
        lcum_t = lcum2.T
        dt_t = dt[prow].T
        e = 0 if reverse else CHUNK - 1

        def cb_product(cc, g):
            crow = slice(pb * 128 + cc * CHUNK, pb * 128 + (cc + 1) * CHUNK)
            gs = slice(g * MA_STATE, (g + 1) * MA_STATE)
            return lax.dot_general(cm[crow][:, gs], bm[crow][:, gs], NT_DIMS, preferred_element_type=F32)

        def chunk_pre(cc, with_cb):
            crow = slice(pb * 128 + cc * CHUNK, pb * 128 + (cc + 1) * CHUNK)
            lc = lcum2[cc * CHUNK:(cc + 1) * CHUNK]
            llast = lc[e:e + 1]
            p_ = {"w_out": _expand(dt[crow] * jnp.exp(llast - lc), e_mat),
                  "expl": _expand(jnp.broadcast_to(jnp.exp(llast), (8, 128)), e_mat)[0:1]}
            if need_y:
                p_["dec_in"] = _expand(jnp.exp(lc), e_mat)
                if with_cb:
                    p_["cbg"] = [cb_product(cc, g) for g in range(MA_GROUPS)]
            return p_

        hoist = mode == "rev_out"
        pre = {cc: chunk_pre(cc, True) for cc in range(2)} if hoist else {}
        for ci in range(2):
            cc = 1 - ci if reverse else ci
            if not hoist:
                pre[cc] = chunk_pre(cc, False)
            crow = slice(pb * 128 + cc * CHUNK, pb * 128 + (cc + 1) * CHUNK)
            ccol = slice(cc * CHUNK, (cc + 1) * CHUNK)
            lc = lcum2[ccol]
            xh_c = xh[crow]
            xb_c = xh_c.astype(BF16)
            xd = (xh_c * pre[cc]["w_out"]).astype(BF16)
            expl = pre[cc]["expl"]
            bc = bm[crow]
            if need_y:
                dec_in = pre[cc]["dec_in"]
                cc_ = cm[crow]
            ys = []
            for g in range(MA_GROUPS):
                gs = slice(g * MA_STATE, (g + 1) * MA_STATE)
                gw = slice(g * MA_GW, (g + 1) * MA_GW)
                hg = hst_ref[g]
                if need_y:
                    cbg = pre[cc]["cbg"][g] if hoist else cb_product(cc, g)
                    pair_out = []
                    for jp in range(2):
                        ms = []
                        for jj in range(2):
                            ln = lane0 + 4 * g + 2 * jp + jj
                            diff = lc[:, ln:ln + 1] - lcum_t[ln:ln + 1, ccol]
                            lmat = jnp.exp(jnp.where(mask, diff, NEG))
                            ms.append(cbg * lmat * dt_t[ln:ln + 1, ccol])
                        lhs = jnp.concatenate(ms, axis=0).astype(BF16)
                        xs = xb_c[:, g * MA_GW + jp * 128:g * MA_GW + (jp + 1) * 128]
                        res = jnp.dot(lhs, xs, preferred_element_type=F32)
                        pair_out.append(jnp.where(lane128 < MA_HEAD_DIM, res[:CHUNK], res[CHUNK:]))
                    y_intra = jnp.concatenate(pair_out, axis=1)
                    y_inter = jnp.dot(cc_[:, gs], hg.astype(BF16), preferred_element_type=F32)
                    ys.append(y_intra + y_inter * dec_in[:, gw])
                upd = lax.dot_general(bc[:, gs], xd[:, gw], TN_DIMS, preferred_element_type=F32)
                hst_ref[g] = expl[:, gw] * hg + upd
            if not need_y:
                continue
            y = jnp.concatenate(ys, axis=1)
            if mode == "fwd":
                y_ref[0, 0, crow, :] = y.astype(BF16)
            else:
                ytot = yf_ref[0, 0, crow, :].astype(F32) + y + xh_c * dsk_ref[...]
                zc = jnp.concatenate([z0_ref[0, 0, crow, :], z1_ref[0, 0, crow, :]], axis=1).astype(F32)
                yz = ytot * _silu(zc)
                nw = nw_ref[...]
                parts = []
                for g in range(MA_GROUPS):
                    gw = slice(g * MA_GW, (g + 1) * MA_GW)
                    yg = yz[:, gw]
                    ms_ = jnp.mean(yg * yg, -1, keepdims=True)
                    parts.append(yg * lax.rsqrt(ms_ + LN_EPS) * nw[:, gw])
                ys_ref[crow, :] = jnp.concatenate(parts, axis=1).astype(BF16)

    if mode == "state":
        @pl.when(j == pl.num_programs(1) - 1)
        def _():
            hfin_ref[0] = hst_ref[...]
    elif mode == "rev_out":
        y_ref[0, 0] = jnp.dot(ys_ref[...], wb_ref[...], preferred_element_type=F32)


def _ssd(pcm, dtraw, h0, consts, reverse, mode, y_f=None, xconv=None, extra=None):
    b, ncols, nrows, _ = pcm.shape
    conv_w, conv_b, dt_bias, a128, e_mat = consts

    def wmap(col):
        if reverse:
            return lambda bi, j: (bi, ncols - 1 - j, 0, col)
        return lambda bi, j: (bi, j, 0, col)

    pspec = lambda col: pl.BlockSpec((1, 1, nrows, 1024), wmap(col))
    dspec = pl.BlockSpec((1, 1, nrows, DT_W), wmap(0))
    hspec = pl.BlockSpec((1, MA_GROUPS, MA_STATE, MA_GW), lambda bi, j: (bi, 0, 0, 0))
    cspec = lambda shape: pl.BlockSpec(shape, lambda bi, j: (0,) * len(shape))
    tri = _tri(128, reverse, block=CHUNK)
    kern = functools.partial(_ssd_kernel, reverse=reverse, mode=mode, nrows=nrows)
    base_specs = [pspec(PC_X0), pspec(PC_X1), pspec(PC_B), pspec(PC_C), dspec, hspec,
                  cspec((MA_CONV, 4096)), cspec((1, 4096)), cspec((1, DT_W)), cspec((1, DT_W)),
                  cspec((DT_W, MA_INNER)), cspec((128, 128))]
    xsrc = pcm if xconv is None else xconv
    base_args = [xsrc, xsrc, xsrc, xsrc, dtraw, h0, conv_w, conv_b, dt_bias, a128, e_mat, tri]
    h_scr = pltpu.VMEM((MA_GROUPS, MA_STATE, MA_GW), F32)
    sem = _cparams(("arbitrary", "arbitrary"))
    if mode == "state":
        return pl.pallas_call(
            kern, grid=(b, ncols), in_specs=base_specs, out_specs=hspec,
            out_shape=jax.ShapeDtypeStruct(h0.shape, F32), scratch_shapes=[h_scr],
            compiler_params=sem, name="ssd_state")(*base_args)
    if mode == "fwd":
        nconv = MA_INNER + 2 * MA_GROUPS * MA_STATE
        return pl.pallas_call(
            kern, grid=(b, ncols), in_specs=base_specs,
            out_specs=[pl.BlockSpec((1, 1, nrows, MA_INNER), wmap(0)),
                       pl.BlockSpec((1, 1, nrows, nconv), wmap(0))],
            out_shape=[jax.ShapeDtypeStruct((b, ncols, nrows, MA_INNER), BF16),
                       jax.ShapeDtypeStruct((b, ncols, nrows, nconv), BF16)],
            scratch_shapes=[h_scr], compiler_params=sem, name="ssd_fwd")(*base_args)
    d_skip, norm_w, w_b = extra
    return pl.pallas_call(
        kern, grid=(b, ncols),
        in_specs=base_specs + [pspec(PC_Z0), pspec(PC_Z1),
                               pl.BlockSpec((1, 1, nrows, MA_INNER), wmap(0)),
                               cspec((1, MA_INNER)), cspec((1, MA_INNER)), cspec((MA_INNER, D_MODEL))],
        out_specs=pl.BlockSpec((1, 1, nrows, D_MODEL), wmap(0)),
        out_shape=jax.ShapeDtypeStruct((b, ncols, nrows, D_MODEL), F32),
        scratch_shapes=[h_scr, pltpu.VMEM((nrows, MA_INNER), BF16)],
        compiler_params=sem, name="ssd_rev")(*base_args, pcm, pcm, y_f, d_skip, norm_w, w_b)


def _merge_kernel(x_ref, ya_ref, yb_ref, perm_ref, ga_ref, gb_ref, g1_ref, wo_ref, lg_ref, lb_ref, o_ref):
    tm, d = x_ref.shape
    yb = yb_ref[...].reshape(tm, d)
    hi = yb.astype(BF16)
    lo = (yb - hi.astype(F32)).astype(BF16)
    perm = perm_ref[...]
    yb = (jnp.dot(perm, hi, preferred_element_type=F32) + jnp.dot(perm, lo, preferred_element_type=F32))
    m = (_sigmoid(ga_ref[...].astype(F32)) * ya_ref[...] + _sigmoid(gb_ref[...].astype(F32)) * yb)
    out = jnp.dot(m.astype(BF16), wo_ref[...], preferred_element_type=F32)
    z = DN_ALPHA * x_ref[...] + g1_ref[0] * out
    o_ref[...] = _layer_norm(z) * lg_ref[...] + lb_ref[...]


def _merge(x2, y_a, y_b_cm, proj_rm, g1, w_out, ln_g, ln_b, tokens_per_batch):
    m, d = x2.shape
    ncols = y_b_cm.shape[1]
    tm = COL_TILE * ncols
    tpb = tokens_per_batch // tm
    tspec = pl.BlockSpec((tm, d), lambda i: (i, 0))
    cspec = lambda shape: pl.BlockSpec(shape, lambda i: (0,) * len(shape))
    perm = jnp.asarray(_perm_matrix(ncols, COL_TILE)).astype(BF16)
    return pl.pallas_call(
        _merge_kernel, grid=(m // tm,),
        in_specs=[tspec, tspec,
                  pl.BlockSpec((1, ncols, COL_TILE, d), lambda i: (i // tpb, 0, i % tpb, 0)),
                  cspec((tm, tm)),
                  pl.BlockSpec((tm, d), lambda i: (i, PB_GA)),
                  pl.BlockSpec((tm, d), lambda i: (i, PB_GB)),
                  pl.BlockSpec((1, 1, d), lambda i: (i // tpb, 0, 0)),
                  cspec((d, d)), cspec((1, d)), cspec((1, d))],
        out_specs=tspec,
        out_shape=jax.ShapeDtypeStruct((m, d), F32),
        compiler_params=_cparams(("arbitrary",)),
        name="merge",
    )(x2, y_a, y_b_cm, perm, proj_rm, proj_rm, g1, w_out, ln_g, ln_b)


KEY_BIAS = 0x40000000
KEY_MIN = 1.1754944e-38
LOW_MASK = PEER_NKEYS - 1


def _sort_pairs(n):
    def merge(lo, hi, r):
        step = r * 2
        if step < hi - lo:
            yield from merge(lo, hi, step)
            yield from merge(lo + r, hi, step)
            for i in range(lo + r, hi - r, step):
                yield (i, i + r)
        else:
            yield (lo, lo + r)

    def sort(lo, hi):
        if hi - lo >= 1:
            mid = lo + (hi - lo) // 2
            yield from sort(lo, mid)
            yield from sort(mid + 1, hi)
            yield from merge(lo, hi, 1)

    return list(sort(0, n - 1))


def _merge_pairs(n):
    out, j = [], n // 2
    while j:
        out += [(i, i | j) for i in range(n) if not i & j]
        j //= 2
    return out


def _cex(v, pairs):
    v = list(v)
    for i, j in pairs:
        v[i], v[j] = jnp.maximum(v[i], v[j]), jnp.minimum(v[i], v[j])
    return v


def _cex_p(v, c, pairs):
    v, c = list(v), list(c)
    for i, j in pairs:
        m = v[i] >= v[j]
        v[i], v[j], c[i], c[j] = (jnp.maximum(v[i], v[j]), jnp.minimum(v[i], v[j]),
                                  jnp.where(m, c[i], c[j]), jnp.where(m, c[j], c[i]))
    return v, c


def _to_key(s, low):
    b = pltpu.bitcast(s, I32)
    k = jnp.where(b < 0, jnp.bitwise_xor(b, 0x7FFFFFFF), b)
    k = jnp.right_shift(k, 1) + KEY_BIAS
    k = jnp.bitwise_or(jnp.bitwise_and(k, ~LOW_MASK), low)
    return pltpu.bitcast(k, F32)


def _from_key(kf):
    k = pltpu.bitcast(kf, I32)
    low = jnp.bitwise_and(k, LOW_MASK)
    k = jnp.left_shift(jnp.bitwise_and(k, ~LOW_MASK) - KEY_BIAS, 1)
    b = jnp.where(k < 0, jnp.bitwise_xor(k, 0x7FFFFFFF), k)
    return pltpu.bitcast(b, F32), low


def _top16_keys(st, sub):
    v = []
    for r in range(PEER_NKEYS // 8):
        v.append(_to_key(st[8 * r:8 * r + 8], (LOW_MASK - 8 * r) - sub))
    v = _cex(v, _sort_pairs(16))
    for sh in (4, 2, 1):
        part = [pltpu.roll(x, sh, axis=0) for x in v]
        v = _cex([jnp.maximum(v[r], part[15 - r]) for r in range(16)], _merge_pairs(16))
    return v


def _natural(lst, half, sub):
    x = lst[8 * half]
    for s_ in range(1, 8):
        x = jnp.where(sub == s_, lst[8 * half + s_], x)
    return x


def _slot_table():
    slots = {}
    for s_ in range(8):
        slots[(0, s_)] = (0, s_)
        slots[(1, s_)] = (0, 8 + s_)
        slots[(3, s_)] = (8 + s_, 0)
    for s_ in range(1, 8):
        slots[(2, s_)] = (s_, 0)
        slots[(4, s_)] = (1, s_)
    for s_ in range(2, 8):
        slots[(5, s_)] = (s_, 1)
    for s_ in (2, 3, 4):
        slots[(6, s_)] = (2, s_)
    for s_ in (2, 3):
        slots[(7, s_)] = (3, s_)
    slots[(7, 4)] = (4, 2)
    want = sorted((j, l) for j in range(PEER_TOPK) for l in range(PEER_TOPK) if (j + 1) * (l + 1) <= PEER_TOPK)
    assert sorted(slots.values()) == want
    low = -np.ones((8, 8), np.int32)
    for (q, s_), jl in slots.items():
        low[q, s_] = LOW_MASK - want.index(jl)
    return jnp.asarray(np.repeat(low.reshape(64, 1), 128, axis=1))


def _retrieve_head(q_ref, sk_ref, slot_low, sub, t0, h):
    vals, idxs = [], []
    for i in range(2):
        cidx = 2 * h + i
        qc = q_ref[pl.ds(t0, 128), cidx * PEER_HALF:(cidx + 1) * PEER_HALF]
        st = lax.dot_general(sk_ref[cidx], qc, NT_DIMS, preferred_element_type=F32)
        dec = [_from_key(k) for k in _top16_keys(st, sub)]
        vals.append([d[0] for d in dec])
        idxs.append([LOW_MASK - d[1] for d in dec])
    v1, v2 = vals
    i1, i2 = idxs
    s1n = [_natural(v1, hf, sub) for hf in range(2)]
    s2n = [_natural(v2, hf, sub) for hf in range(2)]
    i1n = [_natural(i1, hf, sub) * PEER_NKEYS for hf in range(2)]
    i2n = [_natural(i2, hf, sub) for hf in range(2)]
    row4 = sub == 4
    cand = [(v1[0] + s2n[0], i1[0] * PEER_NKEYS + i2n[0]),
            (v1[0] + s2n[1], i1[0] * PEER_NKEYS + i2n[1]),
            (s1n[0] + v2[0], i1n[0] + i2[0]),
            (s1n[1] + v2[0], i1n[1] + i2[0]),
            (v1[1] + s2n[0], i1[1] * PEER_NKEYS + i2n[0]),
            (s1n[0] + v2[1], i1n[0] + i2[1]),
            (v1[2] + s2n[0], i1[2] * PEER_NKEYS + i2n[0]),
            (jnp.where(row4, s1n[0] + v2[2], v1[3] + s2n[0]),
             jnp.where(row4, i1n[0] + i2[2], i1[3] * PEER_NKEYS + i2n[0]))]
    keys, codes = [], []
    for qi, (cv, cc) in enumerate(cand):
        low = slot_low[8 * qi:8 * qi + 8]
        keys.append(jnp.where(low >= 0, _to_key(cv, low), KEY_MIN))
        codes.append(cc)
    keys, codes = _cex_p(keys, codes, _sort_pairs(8))
    pk = [pltpu.roll(x, 4, axis=0) for x in keys]
    pc = [pltpu.roll(x, 4, axis=0) for x in codes]
    keys, codes = _cex_p(keys + pk[::-1], codes + pc[::-1], _merge_pairs(16))
    for sh in (2, 1):
        pk = [pltpu.roll(x, sh, axis=0) for x in keys]
        pc = [pltpu.roll(x, sh, axis=0) for x in codes]
        mk, mc = [], []
        for r in range(16):
            m = keys[r] >= pk[15 - r]
            mk.append(jnp.maximum(keys[r], pk[15 - r]))
            mc.append(jnp.where(m, codes[r], pc[15 - r]))
        keys, codes = _cex_p(mk, mc, _merge_pairs(16))
    tv = [_from_key(k)[0] for k in keys]
    ex = [jnp.exp(t - tv[0]) for t in tv]
    den = ex[0]
    for e_ in ex[1:]:
        den = den + e_
    inv = 1.0 / den
    return codes, [e_ * inv for e_ in ex]


def _peer_kernel(x_ref, sh_ref, sc_ref, g2_ref, wq_ref, sk_ref, slot_ref, u_ref, v_ref, lg_ref, lb_ref,
                 o_ref, xm_ref, q_ref, acc_ref, t_ref, tc_ref, tg_ref, ai_ref, bi_ref, gt_ref):
    s = pl.program_id(1)
    p = x_ref.shape[0]

    @pl.when(s == 0)
    def _():
        xm = (_layer_norm(x_ref[...]) * (1.0 + sc_ref[0]) + sh_ref[0]).astype(BF16)
        xm_ref[...] = xm
        acc_ref[...] = jnp.zeros_like(acc_ref)
        q_ref[...] = jnp.dot(xm, wq_ref[...], preferred_element_type=F32).astype(BF16)
        slot_low = slot_ref[...]
        sub = lax.broadcasted_iota(I32, (8, 128), 0)

        def tile(ti, carry):
            t0 = pl.multiple_of(ti * 128, 128)
            for h in range(PEER_HEADS):
                codes, gates = _retrieve_head(q_ref, sk_ref, slot_low, sub, t0, h)
                for r in range(PEER_TOPK):
                    row = h * PEER_TOPK + r
                    tc_ref[row:row + 1, :] = codes[r][0:1].astype(F32)
                    tg_ref[row:row + 1, :] = gates[r][0:1]
            ct = tc_ref[...].T.astype(I32)
            ai_ref[pl.ds(t0, 128), :] = jnp.right_shift(ct, 7).astype(F32)
            bi_ref[pl.ds(t0, 128), :] = jnp.bitwise_and(ct, LOW_MASK).astype(F32)
            gt_ref[pl.ds(t0, 128), :] = tg_ref[...].T
            return carry

        lax.fori_loop(0, p // 128, tile, 0)

        sub128 = lax.broadcasted_iota(I32, (PEER_NKEYS, 128), 0)
        amap = jnp.bitwise_or(jnp.left_shift(jnp.bitwise_and(sub128, 63), 1),
                              jnp.right_shift(sub128, 6)).astype(F32)
        bmap = sub128.astype(F32)

        def tok8(i, carry):
            t0 = pl.multiple_of(i * TOK_UNROLL, TOK_UNROLL)
            a8 = ai_ref[pl.ds(t0, TOK_UNROLL), :]
            b8 = bi_ref[pl.ds(t0, TOK_UNROLL), :]
            g8 = gt_ref[pl.ds(t0, TOK_UNROLL), :]
            for j in range(TOK_UNROLL):
                at = jnp.where(a8[j:j + 1] == amap, 1.0, 0.0).astype(BF16)
                rt = jnp.where(b8[j:j + 1] == bmap, g8[j:j + 1], 0.0).astype(BF16)
                w = lax.dot_general(at, rt, NT_DIMS, preferred_element_type=F32)
                lo = pltpu.bitcast(w[:64], I32) + 0x8000
                hi = pltpu.bitcast(w[64:], I32) + 0x8000
                lo = lax.shift_right_logical(lo, 16)
                hi = jnp.bitwise_and(hi, -65536)
                r0 = pl.multiple_of((t0 + j) * PEER_PITCH, 8)
                t_ref[pl.ds(r0, 64), :] = jnp.bitwise_or(lo, hi)
            return carry

        lax.fori_loop(0, p // TOK_UNROLL, tok8, 0)

    hval = lax.dot_general(xm_ref[...], u_ref[...], NT_DIMS, preferred_element_type=F32)
    act = 0.5 * hval * (1.0 + lax.erf(hval * 0.7071067811865476))
    wgts = []
    for g in range(PEER_G):
        wp = t_ref[pl.ds(s * PEER_G + g, p, stride=PEER_PITCH), :]
        wgts.append(pltpu.bitcast(jnp.left_shift(wp, 16), F32))
        wgts.append(pltpu.bitcast(jnp.bitwise_and(wp, -65536), F32))
    z = (jnp.concatenate(wgts, axis=1) * act).astype(BF16)
    acc_ref[...] += jnp.dot(z, v_ref[...], preferred_element_type=F32)

    @pl.when(s == pl.num_programs(1) - 1)
    def _():
        z = DN_ALPHA * x_ref[...] + g2_ref[0] * acc_ref[...]
        o_ref[...] = _layer_norm(z) * lg_ref[...] + lb_ref[...]


def _peer(x1, shift, scale, g2, wq, sk, u2, v2, ln_g, ln_b, tokens_per_batch):
    m, d = x1.shape
    p = PEER_P
    tpb = tokens_per_batch // p
    blk = PEER_G * 2 * PEER_NKEYS
    nsteps = u2.shape[0] // blk
    slot = _slot_table()
    cspec = lambda shape: pl.BlockSpec(shape, lambda i, s: (0,) * len(shape))
    bspec = pl.BlockSpec((1, 1, d), lambda i, s: (i // tpb, 0, 0))
    nent = PEER_HEADS * PEER_TOPK
    return pl.pallas_call(
        _peer_kernel, grid=(m // p, nsteps),
        in_specs=[pl.BlockSpec((p, d), lambda i, s: (i, 0)), bspec, bspec, bspec,
                  cspec(wq.shape), cspec(sk.shape), cspec(slot.shape),
                  pl.BlockSpec((blk, d), lambda i, s: (s, 0)),
                  pl.BlockSpec((blk, d), lambda i, s: (s, 0)),
                  cspec((1, d)), cspec((1, d))],
        out_specs=pl.BlockSpec((p, d), lambda i, s: (i, 0)),
        out_shape=jax.ShapeDtypeStruct((m, d), F32),
        scratch_shapes=[pltpu.VMEM((p, d), BF16), pltpu.VMEM((p, 2 * PEER_HEADS * PEER_HALF), BF16),
                        pltpu.VMEM((p, d), F32), pltpu.VMEM((p * PEER_PITCH, 128), I32),
                        pltpu.VMEM((nent, 128), F32), pltpu.VMEM((nent, 128), F32),
                        pltpu.VMEM((p, nent), F32), pltpu.VMEM((p, nent), F32), pltpu.VMEM((p, nent), F32)],
        compiler_params=_cparams(("arbitrary", "arbitrary")),
        name="peer",
    )(x1, shift, scale, g2, wq, sk, slot, u2, v2, ln_g, ln_b)


def _layer(x, c, ctx, c_ctx, lw, lb_l):
    b, t, d = x.shape
    tc = ctx.shape[1]

    npad = -(-(b + 1) // 8) * 8
    c_all = jnp.zeros((npad, d), F32).at[:b].set(c).at[b].set(c_ctx)
    mod = _ada(c_all, lw["w_ada"], lw["b_ada"])
    sh1, sc1, g1, sh2, sc2, g2 = [mod[:b, i * d:(i + 1) * d].reshape(b, 1, d) for i in range(6)]
    csh1 = jnp.broadcast_to(mod[b, 0:d], (b, 1, d))
    csc1 = jnp.broadcast_to(mod[b, d:2 * d], (b, 1, d))

    w_in = lw["w_in"]
    off = np.cumsum([0, HG_WIDTH, HG_WIDTH, HG_WIDTH, MA_INNER + MA_GROUPS * MA_STATE, MA_HEADS, MA_HEADS,
                     HG_WIDTH, HG_WIDTH, MA_GROUPS * MA_STATE, MA_INNER, D_MODEL, D_MODEL])
    o_ff, o_fb, o_iv, o_xb, o_dtf, o_dtb, o_q, o_og, o_cm, o_z, o_ga, o_gb = off[:12]
    col = lambda o, n: w_in[:, o:o + n]
    w_rm = jnp.concatenate([col(o_ff, 1024), col(o_fb, 1024), col(o_iv, 1024), col(o_q, 1024),
                            col(o_og, 1024), col(o_ga, 1024), col(o_gb, 1024)], axis=1).astype(BF16)
    w_cm = jnp.concatenate([col(o_xb, 3072), col(o_cm, 1024), col(o_z, 2048)], axis=1).astype(BF16)
    w_dt = jnp.concatenate([col(o_dtf, 2 * MA_HEADS), jnp.zeros((d, DT_W - 2 * MA_HEADS), F32)],
                           axis=1).astype(BF16)
    lb_f = lb_l[0].reshape(1, HG_WIDTH)
    lb_b = lb_l[1].reshape(1, HG_WIDTH)
    zpad = jnp.zeros((DT_W - 2 * MA_HEADS,), F32)
    dt_bias = jnp.concatenate([lw["ma_dt_bias"].reshape(-1), zpad]).reshape(1, DT_W)
    a128 = jnp.concatenate([-jnp.exp(lw["ma_a_log"].astype(F32)).reshape(-1), zpad]).reshape(1, DT_W)
    conv_w = lw["ma_conv_w"].T
    conv_b = lw["ma_conv_b"].reshape(1, -1)
    head_of = np.arange(MA_INNER) // MA_HEAD_DIM
    e_f = jnp.asarray((np.arange(DT_W)[:, None] == head_of[None, :]).astype(np.float32)).astype(BF16)
    e_b = jnp.asarray((np.arange(DT_W)[:, None] == (head_of + MA_HEADS)[None, :]).astype(np.float32)).astype(BF16)
    consts_f = (conv_w, conv_b, dt_bias, a128, e_f)
    consts_b = (conv_w, conv_b, dt_bias, a128, e_b)
    d_skip = jnp.repeat(lw["ma_d"], MA_HEAD_DIM).reshape(1, MA_INNER)
    ma_norm_w = lw["ma_norm_w"].reshape(1, MA_INNER)
    hg_norm_w = lw["hg_norm_w"].reshape(1, HG_DK)
    w_a = lw["w_branch_a"].astype(BF16)
    w_b = lw["w_branch_b"].astype(BF16)
    w_o = lw["w_out"].astype(BF16)

    projc = _inproj_rm(ctx.reshape(b * tc, d), csh1, csc1, w_rm, tc, tc).reshape(b, tc, NP_RM)
    pcmc, dtc = _inproj_cm(ctx, csh1, csc1, w_cm, w_dt, 1)
    zs = jnp.zeros((b, HG_HEADS, HG_DK, HG_DK), F32)
    zh = jnp.zeros((b, MA_GROUPS, MA_STATE, MA_GW), F32)
    s_hf = _hgrn(projc, zs, lb_f, False, "state", tc)
    s_hb = _hgrn(projc, zs, lb_b, True, "state", tc)
    h_f = _ssd(pcmc, dtc, zh, consts_f, False, "state")
    h_b = _ssd(pcmc, dtc, zh, consts_b, True, "state")

    x2 = x.reshape(b * t, d)
    proj = _inproj_rm(x2, sh1, sc1, w_rm, t, min(1024, t))
    pcm, dtr = _inproj_cm(x, sh1, sc1, w_cm, w_dt, GRID_W)
    proj3 = proj.reshape(b, t, NP_RM)
    tt = min(512, t)
    o_f = _hgrn(proj3, s_hf, lb_f, False, "fwd", tt)
    y_a = _hgrn(proj3, s_hb, lb_b, True, "rev_out", tt, o_prev=o_f, norm_w=hg_norm_w, w_a=w_a)
    y_f, xconv = _ssd(pcm, dtr, h_f, consts_f, False, "fwd")
    y_b = _ssd(pcm, dtr, h_b, consts_b, True, "rev_out", y_f=y_f, xconv=xconv,
               extra=(d_skip, ma_norm_w, w_b))
    x1 = _merge(x2, y_a.reshape(b * t, d), y_b, proj, g1, w_o,
                lw["ln1_g"].reshape(1, d), lw["ln1_b"].reshape(1, d), t)

    wq = lw["peer_wq"].astype(BF16)
    sk = lw["peer_subkeys"].reshape(PEER_HEADS * 2, PEER_NKEYS, PEER_HALF).astype(BF16)
    u2 = lw["peer_u"].astype(BF16)
    v2 = lw["peer_v"].astype(BF16)
    out = _peer(x1, sh2, sc2, g2, wq, sk, u2, v2, lw["ln2_g"].reshape(1, d), lw["ln2_b"].reshape(1, d), t)
    return out.reshape(b, t, d)


def kernel(x, c, ctx, c_ctx, w_ada, b_ada, w_in, hg_lb_logits, hg_norm_w, ma_conv_w, ma_conv_b, ma_dt_bias,
           ma_a_log, ma_d, ma_norm_w, w_branch_a, w_branch_b, w_out, ln1_g, ln1_b, peer_wq, peer_subkeys,
           peer_u, peer_v, ln2_g, ln2_b):
    depth = w_in.shape[0]
    assert depth == 1, "the scan states of a single (last) layer are implemented"
    lb_all = jnp.cumsum(jax.nn.softmax(hg_lb_logits.astype(F32), axis=1), axis=1)
    lw = {"w_ada": w_ada[0], "b_ada": b_ada[0], "w_in": w_in[0], "hg_norm_w": hg_norm_w[0],
          "ma_conv_w": ma_conv_w[0], "ma_conv_b": ma_conv_b[0], "ma_dt_bias": ma_dt_bias[0],
          "ma_a_log": ma_a_log[0], "ma_d": ma_d[0], "ma_norm_w": ma_norm_w[0],
          "w_branch_a": w_branch_a[0], "w_branch_b": w_branch_b[0], "w_out": w_out[0],
          "ln1_g": ln1_g[0], "ln1_b": ln1_b[0], "peer_wq": peer_wq[0], "peer_subkeys": peer_subkeys[0],
          "peer_u": peer_u[0], "peer_v": peer_v[0], "ln2_g": ln2_g[0], "ln2_b": ln2_b[0]}
    return _layer(x, c, ctx, c_ctx, lw, lb_all[:, 0])
```
